```python
import math
import jax, jax.numpy as jnp
from jax import lax
import numpy as np

D_MODEL = 2048
BATCH = 2
SEQ = 8192
DEPTH = 4

N_MIXERS = 3
N_NSA = (DEPTH + 2) // 3
N_CONV = (DEPTH + 1) // 3
N_GDN = DEPTH // 3
N_DENSE = (DEPTH + 1) // 2
N_MOE = DEPTH // 2

ALPHA = (2 * DEPTH) ** 0.25
BETA = (8 * DEPTH) ** -0.25
LN_EPS = 1e-5
NEG_INF = -1e30

REL_BUCKETS = 32
REL_EXACT = 16
REL_MAX_DIST = 1024

NSA_HEADS = 16
NSA_GROUPS = 4
NSA_HPG = NSA_HEADS // NSA_GROUPS
NSA_HEAD_DIM = 128
CMP_BLOCK = 32
CMP_STRIDE = 16
CMP_HIDDEN = 256
SLC_BLOCK = 64
SLC_TOP_N = 16
WINDOW = 512
Q_BLOCK = 128
NSA_Q_DIM = NSA_HEADS * NSA_HEAD_DIM
NSA_KV_DIM = 3 * 2 * NSA_GROUPS * NSA_HEAD_DIM
NSA_IN_DIM = NSA_Q_DIM + NSA_KV_DIM + 3 * NSA_HEADS
FORCE_SCORE = 1e6

CONV_WIDTH = 31

GDN_QK_HEADS = 16
GDN_V_HEADS = 32
GDN_HEAD_DIM = 128
GDN_CONV = 4
GDN_CHUNK = 64
GDN_QK_DIM = GDN_QK_HEADS * GDN_HEAD_DIM
GDN_V_DIM = GDN_V_HEADS * GDN_HEAD_DIM
GDN_QKV_DIM = 2 * GDN_QK_DIM + GDN_V_DIM
GDN_IN_DIM = GDN_QKV_DIM + GDN_V_DIM + 2 * GDN_V_HEADS

D_FF = 5632
N_EXPERTS = 8
TOP_K = 2
D_FF_EXPERT = 4096

kernel_name = 'hybrid_nsa_conformer_gdn_moe_deepnorm_adaln'


def layer_norm(x, g, b):
    xf = x.astype(jnp.float32)
    mu = xf.mean(-1, keepdims=True)
    var = jnp.square(xf - mu).mean(-1, keepdims=True)
    return ((xf - mu) * lax.rsqrt(var + LN_EPS) * g + b).astype(x.dtype)


def masked_softmax(s, valid):
    p = jax.nn.softmax(jnp.where(valid, s, NEG_INF), axis=-1)
    return jnp.where(valid, p, 0.0)


def rel_bucket(dist):
    n = jnp.maximum(dist, 0)
    large = REL_EXACT + (jnp.log(jnp.maximum(n, 1).astype(jnp.float32) / REL_EXACT)
                         / math.log(REL_MAX_DIST / REL_EXACT) * (REL_BUCKETS - REL_EXACT)).astype(jnp.int32)
    return jnp.where(n < REL_EXACT, n, jnp.minimum(large, REL_BUCKETS - 1))


def causal_dwconv(x, w):
    K = w.shape[0]
    xp = jnp.pad(x, ((0, 0), (K - 1, 0), (0, 0)))
    return lax.conv_general_dilated(xp, w[:, None, :].astype(x.dtype), window_strides=(1,), padding='VALID',
                                    dimension_numbers=('NWC', 'WIO', 'NWC'), feature_group_count=x.shape[-1])


def swiglu(x, w1, w3, w2):
    return (jax.nn.silu(x @ w1) * (x @ w3)) @ w2


def nsa_mixer(h, w_in, cmp_pos, cmp_w1k, cmp_w2k, cmp_w1v, cmp_w2v, w_o, rel_bias):
    B, T, _ = h.shape
    G, HPG, dh = NSA_GROUPS, NSA_HPG, NSA_HEAD_DIM
    scale = dh ** -0.5
    proj = h @ w_in
    q = proj[..., :NSA_Q_DIM].reshape(B, T, G, HPG, dh)
    kv = proj[..., NSA_Q_DIM:NSA_Q_DIM + NSA_KV_DIM].reshape(B, T, 3, 2, G, dh)
    gates = jax.nn.sigmoid(proj[..., NSA_Q_DIM + NSA_KV_DIM:].astype(jnp.float32)).reshape(B, T, G, HPG, 3)

    n_cmp = (T - CMP_BLOCK) // CMP_STRIDE + 1
    cmp_start = jnp.arange(n_cmp) * CMP_STRIDE
    blk_idx = cmp_start[:, None] + jnp.arange(CMP_BLOCK)[None, :]

    def compress(t, pos, w1, w2):
        tb = t[:, blk_idx] + pos[None, None, :, None, :]
        tb = tb.transpose(0, 1, 3, 2, 4).reshape(B, n_cmp, G, CMP_BLOCK * dh)
        return jax.nn.silu(tb @ w1) @ w2

    k_cmp = compress(kv[:, :, 0, 0], cmp_pos[0], cmp_w1k, cmp_w2k)
    v_cmp = compress(kv[:, :, 0, 1], cmp_pos[1], cmp_w1v, cmp_w2v)
    cmp_end = cmp_start + CMP_BLOCK - 1
    cmp_ctr = cmp_start + CMP_BLOCK // 2

    n_slc = T // SLC_BLOCK
    n_sel = min(SLC_TOP_N, n_slc)
    k_slc = kv[:, :, 1, 0].reshape(B, n_slc, SLC_BLOCK, G, dh).transpose(0, 3, 1, 2, 4)
    v_slc = kv[:, :, 1, 1].reshape(B, n_slc, SLC_BLOCK, G, dh).transpose(0, 3, 1, 2, 4)
    slc_start = jnp.arange(n_slc) * SLC_BLOCK
    overlap = ((cmp_start[:, None] < slc_start[None, :] + SLC_BLOCK)
               & (cmp_start[:, None] + CMP_BLOCK > slc_start[None, :])).astype(jnp.float32)

    pad = ((0, 0), (WINDOW, 0), (0, 0), (0, 0))
    k_win = jnp.pad(kv[:, :, 2, 0], pad)
    v_win = jnp.pad(kv[:, :, 2, 1], pad)

    rb = rel_bias.astype(jnp.float32).reshape(REL_BUCKETS, G, HPG)
    rb_g = rb.transpose(1, 0, 2)
    b_ix = jnp.arange(B)[:, None, None, None]
    g_ix = jnp.arange(G)[None, :, None, None]

    def q_block(qi):
        qs = qi * Q_BLOCK
        tq = qs + jnp.arange(Q_BLOCK)
        qb = lax.dynamic_slice_in_dim(q, qs, Q_BLOCK, axis=1)
        gb = lax.dynamic_slice_in_dim(gates, qs, Q_BLOCK, axis=1)
        s = jnp.einsum('bqghd,bcgd->bghqc', qb, k_cmp).astype(jnp.float32) * scale
        s = s + rb[rel_bucket(tq[:, None] - cmp_ctr[None, :])].transpose(2, 3, 0, 1)
        p_cmp = masked_softmax(s, cmp_end[None, :] <= tq[:, None])
        o_cmp = jnp.einsum('bghqc,bcgd->bqghd', p_cmp.astype(v_cmp.dtype), v_cmp)
        imp = jnp.einsum('bghqc,cs->bgqs', p_cmp, overlap)
        blk = jnp.arange(n_slc)[None, :]
        cur = (tq // SLC_BLOCK)[:, None]
        forced = (blk == 0) | (blk == cur) | (blk == cur - 1)
        imp = jnp.where(forced, FORCE_SCORE, jnp.where(blk <= cur, imp, -1.0))
        _, sel = lax.top_k(imp, n_sel)
        ks = k_slc[b_ix, g_ix, sel].reshape(B, G, Q_BLOCK, n_sel * SLC_BLOCK, dh)
        vs = v_slc[b_ix, g_ix, sel].reshape(B, G, Q_BLOCK, n_sel * SLC_BLOCK, dh)
        kpos = (sel[..., None] * SLC_BLOCK + jnp.arange(SLC_BLOCK)).reshape(B, G, Q_BLOCK, n_sel * SLC_BLOCK)
        dist = tq[None, None, :, None] - kpos
        s = jnp.einsum('bqghd,bgqkd->bghqk', qb, ks).astype(jnp.float32) * scale
        s = s + rb_g[g_ix, rel_bucket(dist)].transpose(0, 1, 4, 2, 3)
        p_slc = masked_softmax(s, (dist >= 0)[:, :, None])
        o_slc = jnp.einsum('bghqk,bgqkd->bqghd', p_slc.astype(vs.dtype), vs)
        kw = lax.dynamic_slice_in_dim(k_win, qs, WINDOW + Q_BLOCK, axis=1)
        vw = lax.dynamic_slice_in_dim(v_win, qs, WINDOW + Q_BLOCK, axis=1)
        kpos_w = qs - WINDOW + jnp.arange(WINDOW + Q_BLOCK)
        dist_w = tq[:, None] - kpos_w[None, :]
        valid_w = (dist_w >= 0) & (dist_w < WINDOW) & (kpos_w[None, :] >= 0)
        s = jnp.einsum('bqghd,bkgd->bghqk', qb, kw).astype(jnp.float32) * scale
        s = s + rb[rel_bucket(dist_w)].transpose(2, 3, 0, 1)
        p_win = masked_softmax(s, valid_w)
        o_win = jnp.einsum('bghqk,bkgd->bqghd', p_win.astype(vw.dtype), vw)
        o = gb[..., 0:1] * o_cmp + gb[..., 1:2] * o_slc + gb[..., 2:3] * o_win
        return o.reshape(B, Q_BLOCK, NSA_Q_DIM).astype(h.dtype)

    out = lax.map(q_block, jnp.arange(T // Q_BLOCK))
    out = out.transpose(1, 0, 2, 3).reshape(B, T, NSA_Q_DIM)
    return out @ w_o


def conformer_conv(h, w_in, dw, ln_g, ln_b, w_out):
    a = h @ w_in
    u = a[..., :D_MODEL] * jax.nn.sigmoid(a[..., D_MODEL:])
    u = layer_norm(causal_dwconv(u, dw), ln_g, ln_b)
    return jax.nn.silu(u) @ w_out


def l2norm(t):
    tf = t.astype(jnp.float32)
    return tf * lax.rsqrt(jnp.sum(tf * tf, -1, keepdims=True) + 1e-6)


def chunk_gated_delta_rule(q, k, v, g, beta):
    B, T, H, dk = q.shape
    dv = v.shape[-1]
    C = GDN_CHUNK
    N = T // C

    def chunks(t):
        t = t.reshape(B, N, C, H, *t.shape[3:])
        return jnp.moveaxis(t, 2, 3).swapaxes(0, 1)

    lower = jnp.tril(jnp.ones((C, C), dtype=bool))
    strict = jnp.tril(jnp.ones((C, C), dtype=bool), -1)
    eye = jnp.eye(C, dtype=jnp.float32)

    def step(S, inp):
        qc, kc, vc, gc, bc = inp
        gcum = jnp.cumsum(gc, axis=-1)
        decay = jnp.exp(jnp.where(lower, gcum[..., :, None] - gcum[..., None, :], NEG_INF))
        kb = kc * bc[..., None]
        a = jnp.where(strict, jnp.einsum('bhid,bhjd->bhij', kb, kc) * decay, 0.0)
        rhs = jnp.concatenate([vc * bc[..., None], kb * jnp.exp(gcum)[..., None]], axis=-1)
        sol = lax.linalg.triangular_solve(eye + a, rhs, left_side=True, lower=True)
        u, w = sol[..., :dv], sol[..., dv:]
        v_new = u - jnp.einsum('bhck,bhkv->bhcv', w, S)
        attn = jnp.einsum('bhik,bhjk->bhij', qc, kc) * decay
        o = (jnp.einsum('bhck,bhkv->bhcv', qc * jnp.exp(gcum)[..., None], S)
             + jnp.einsum('bhij,bhjv->bhiv', attn, v_new))
        g_last = gcum[..., -1:]
        S = S * jnp.exp(g_last)[..., None] + jnp.einsum('bhck,bhcv->bhkv', kc * jnp.exp(g_last - gcum)[..., None], v_new)
        return S, o

    S0 = jnp.zeros((B, H, dk, dv), jnp.float32)
    _, o = lax.scan(step, S0, (chunks(q), chunks(k), chunks(v), chunks(g), chunks(beta)))
    return o.transpose(1, 0, 3, 2, 4).reshape(B, T, H, dv)


def gated_deltanet(h, w_in, conv_w, a_log, dt_bias, norm_w, w_out):
    B, T, _ = h.shape
    proj = h @ w_in
    qkv = jax.nn.silu(causal_dwconv(proj[..., :GDN_QKV_DIM], conv_w))
    z = proj[..., GDN_QKV_DIM:GDN_QKV_DIM + GDN_V_DIM].reshape(B, T, GDN_V_HEADS, GDN_HEAD_DIM)
    b_raw = proj[..., GDN_QKV_DIM + GDN_V_DIM:GDN_QKV_DIM + GDN_V_DIM + GDN_V_HEADS]
    a_raw = proj[..., GDN_QKV_DIM + GDN_V_DIM + GDN_V_HEADS:]
    rep = GDN_V_HEADS // GDN_QK_HEADS
    q = l2norm(qkv[..., :GDN_QK_DIM].reshape(B, T, GDN_QK_HEADS, GDN_HEAD_DIM))
    k = l2norm(qkv[..., GDN_QK_DIM:2 * GDN_QK_DIM].reshape(B, T, GDN_QK_HEADS, GDN_HEAD_DIM))
    v = qkv[..., 2 * GDN_QK_DIM:].reshape(B, T, GDN_V_HEADS, GDN_HEAD_DIM).astype(jnp.float32)
    q = jnp.repeat(q, rep, axis=2) * GDN_HEAD_DIM ** -0.5
    k = jnp.repeat(k, rep, axis=2)
    beta = jax.nn.sigmoid(b_raw.astype(jnp.float32))
    g = -jnp.exp(a_log.astype(jnp.float32)) * jax.nn.softplus(a_raw.astype(jnp.float32) + dt_bias.astype(jnp.float32))
    o = chunk_gated_delta_rule(q, k, v, g, beta)
    o = (o * lax.rsqrt(jnp.mean(o * o, -1, keepdims=True) + 1e-6) * norm_w.astype(jnp.float32)
         * jax.nn.silu(z.astype(jnp.float32)))
    return o.reshape(B, T, GDN_V_DIM).astype(h.dtype) @ w_out


def moe_swiglu(h, router, w1, w3, w2):
    B, T, D = h.shape
    xt = h.reshape(B * T, D)
    logits = (xt @ router).astype(jnp.float32)
    top_v, top_i = lax.top_k(logits, TOP_K)
    wts = jax.nn.softmax(top_v, axis=-1)
    combine = jnp.einsum('nk,nke->ne', wts, jax.nn.one_hot(top_i, N_EXPERTS, dtype=jnp.float32))
    out = jnp.zeros_like(xt)
    for e in range(N_EXPERTS):
        out = out + combine[:, e:e + 1].astype(xt.dtype) * swiglu(xt, w1[e], w3[e], w2[e])
    return out.reshape(B, T, D)


def setup_inputs(seed: int = 0) -> dict:
    key = jax.random.key(seed)
    ks = iter(jax.random.split(key, 48))

    def nrm(shape, std):
        return std * jax.random.normal(next(ks), shape, jnp.float32)

    D = D_MODEL
    dh = NSA_HEAD_DIM
    a_u = jax.random.uniform(next(ks), (N_GDN, GDN_V_HEADS), jnp.float32, 1.0, 16.0)
    dt = jnp.exp(jax.random.uniform(next(ks), (N_GDN, GDN_V_HEADS), jnp.float32, math.log(1e-3), math.log(1e-1)))
    return {
        'x': nrm((BATCH, SEQ, D), 1.0),
        'c': nrm((BATCH, D), 1.0),
        'rel_bias': nrm((REL_BUCKETS, NSA_HEADS), 0.5),
        'ada_w': nrm((DEPTH, D, 6 * D), 0.1 * D ** -0.5),
        'ada_b': nrm((DEPTH, 6 * D), 0.02),
        'ln_g': 1.0 + nrm((DEPTH, 2, D), 0.02),
        'ln_b': nrm((DEPTH, 2, D), 0.02),
        'nsa_w_in': nrm((N_NSA, D, NSA_IN_DIM), D ** -0.5),
        'nsa_cmp_pos': nrm((N_NSA, 2, CMP_BLOCK, dh), 0.1),
        'nsa_cmp_w1k': nrm((N_NSA, CMP_BLOCK * dh, CMP_HIDDEN), (CMP_BLOCK * dh) ** -0.5),
        'nsa_cmp_w2k': nrm((N_NSA, CMP_HIDDEN, dh), CMP_HIDDEN ** -0.5),
        'nsa_cmp_w1v': nrm((N_NSA, CMP_BLOCK * dh, CMP_HIDDEN), (CMP_BLOCK * dh) ** -0.5),
        'nsa_cmp_w2v': nrm((N_NSA, CMP_HIDDEN, dh), CMP_HIDDEN ** -0.5),
        'nsa_w_o': nrm((N_NSA, NSA_Q_DIM, D), BETA * NSA_Q_DIM ** -0.5),
        'conv_w_in': nrm((N_CONV, D, 2 * D), D ** -0.5),
        'conv_dw': nrm((N_CONV, CONV_WIDTH, D), CONV_WIDTH ** -0.5),
        'conv_ln_g': 1.0 + nrm((N_CONV, D), 0.02),
        'conv_ln_b': nrm((N_CONV, D), 0.02),
        'conv_w_out': nrm((N_CONV, D, D), BETA * D ** -0.5),
        'gdn_w_in': nrm((N_GDN, D, GDN_IN_DIM), D ** -0.5),
        'gdn_conv': nrm((N_GDN, GDN_CONV, GDN_QKV_DIM), GDN_CONV ** -0.5),
        'gdn_a_log': jnp.log(a_u),
        'gdn_dt_bias': dt + jnp.log(-jnp.expm1(-dt)),
        'gdn_norm_w': 1.0 + nrm((N_GDN, GDN_HEAD_DIM), 0.02),
        'gdn_w_out': nrm((N_GDN, GDN_V_DIM, D), BETA * GDN_V_DIM ** -0.5),
        'ffn_w1': nrm((N_DENSE, D, D_FF), D ** -0.5),
        'ffn_w3': nrm((N_DENSE, D, D_FF), D ** -0.5),
        'ffn_w2': nrm((N_DENSE, D_FF, D), BETA * D_FF ** -0.5),
        'moe_router': nrm((N_MOE, D, N_EXPERTS), D ** -0.5),
        'moe_w1': nrm((N_MOE, N_EXPERTS, D, D_FF_EXPERT), D ** -0.5),
        'moe_w3': nrm((N_MOE, N_EXPERTS, D, D_FF_EXPERT), D ** -0.5),
        'moe_w2': nrm((N_MOE, N_EXPERTS, D_FF_EXPERT, D), BETA * D_FF_EXPERT ** -0.5),
    }


def reference(x, c, rel_bias, ada_w, ada_b, ln_g, ln_b,
              nsa_w_in, nsa_cmp_pos, nsa_cmp_w1k, nsa_cmp_w2k, nsa_cmp_w1v, nsa_cmp_w2v, nsa_w_o,
              conv_w_in, conv_dw, conv_ln_g, conv_ln_b, conv_w_out,
              gdn_w_in, gdn_conv, gdn_a_log, gdn_dt_bias, gdn_norm_w, gdn_w_out,
              ffn_w1, ffn_w3, ffn_w2,
              moe_router, moe_w1, moe_w3, moe_w2):
    cond = jax.nn.silu(c)
    for i in range(DEPTH):
        mod = cond @ ada_w[i] + ada_b[i]
        sh1, sc1, g1, sh2, sc2, g2 = jnp.split(mod[:, None, :], 6, axis=-1)
        h = x * (1 + sc1) + sh1
        kind, j = i % N_MIXERS, i // N_MIXERS
        if kind == 0:
            y = nsa_mixer(h, nsa_w_in[j], nsa_cmp_pos[j], nsa_cmp_w1k[j], nsa_cmp_w2k[j],
                          nsa_cmp_w1v[j], nsa_cmp_w2v[j], nsa_w_o[j], rel_bias)
        elif kind == 1:
            y = conformer_conv(h, conv_w_in[j], conv_dw[j], conv_ln_g[j], conv_ln_b[j], conv_w_out[j])
        else:
            y = gated_deltanet(h, gdn_w_in[j], gdn_conv[j], gdn_a_log[j], gdn_dt_bias[j], gdn_norm_w[j], gdn_w_out[j])
        x = layer_norm(ALPHA * x + (1 + g1) * y, ln_g[i, 0], ln_b[i, 0])
        h = x * (1 + sc2) + sh2
        m = i // 2
        if i % 2 == 0:
            y = swiglu(h, ffn_w1[m], ffn_w3[m], ffn_w2[m])
        else:
            y = moe_swiglu(h, moe_router[m], moe_w1[m], moe_w3[m], moe_w2[m])
        x = layer_norm(ALPHA * x + (1 + g2) * y, ln_g[i, 1], ln_b[i, 1])
    return x
```

```python
import functools
import math

import jax
import jax.numpy as jnp
from jax import lax
from jax.experimental import pallas as pl
from jax.experimental.pallas import tpu as pltpu

F32 = jnp.float32
BF16 = jnp.bfloat16

DEPTH = 4
ALPHA = (2 * DEPTH) ** 0.25
LN_EPS = 1e-5
NEG_INF = -1e30

REL_BUCKETS = 32
REL_EXACT = 16
REL_MAX_DIST = 1024

NSA_HEADS = 16
NSA_GROUPS = 4
NSA_HPG = NSA_HEADS // NSA_GROUPS
HEAD_DIM = 128
CMP_BLOCK = 32
CMP_STRIDE = 16
CMP_HIDDEN = 256
SLC_BLOCK = 64
SLC_TOP_N = 16
WINDOW = 512
Q_BLOCK = 128
FORCE_SCORE = 1e6

CONV_WIDTH = 31
GDN_QK_HEADS = 16
GDN_V_HEADS = 32
GDN_CONV = 4
GDN_CHUNK = 64

N_EXPERTS = 8
TOP_K = 2

LANES = 128
SUBLANES = 8
MIB = 1 << 20


def _cparams(sem, vmem_mib=48):
    return pltpu.CompilerParams(dimension_semantics=sem, vmem_limit_bytes=vmem_mib * MIB)


def _dot(a, b):
    return jnp.dot(a, b, preferred_element_type=F32)


def _dot_nt(a, b):
    return lax.dot_general(a, b, (((1,), (1,)), ((), ())), preferred_element_type=F32)


def _split3(a):
    hi = a.astype(BF16)
    r1 = a - hi.astype(F32)
    mid = r1.astype(BF16)
    lo = (r1 - mid.astype(F32)).astype(BF16)
    return hi, mid, lo


def _dot_x3(a, b_bf16):
    hi, mid, lo = _split3(a)
    return _dot(hi, b_bf16) + _dot(mid, b_bf16) + _dot(lo, b_bf16)


def _silu(x):
    return x * (1.0 / (1.0 + jnp.exp(-x)))


def _sigmoid(x):
    return 1.0 / (1.0 + jnp.exp(-x))


def _softplus(x):
    return jnp.maximum(x, 0.0) + jnp.log(1.0 + jnp.exp(-jnp.abs(x)))


def _shr(x, pow2):
    return lax.shift_right_logical(x, jnp.full(x.shape, int(math.log2(pow2)), jnp.int32))


def _adaln_kernel(ct_ref, w_ref, b_ref, o_ref):
    w = w_ref[...]
    tn = w.shape[1]
    rows = []
    for b in range(ct_ref.shape[0]):
        cb = _silu(ct_ref[b])
        parts = [jnp.sum(w[:, j * LANES:(j + 1) * LANES] * cb, axis=0, keepdims=True)
                 for j in range(tn // LANES)]
        rows.append(jnp.concatenate(parts, axis=1))
    o_ref[...] = jnp.concatenate(rows, axis=0) + b_ref[...]


def adaln_all(c, ada_w, ada_b):
    B, D = c.shape
    depth, _, n_out = ada_w.shape
    tn = 1024
    ct = jnp.broadcast_to(c[:, :, None], (B, D, LANES))
    return pl.pallas_call(
        _adaln_kernel,
        grid=(depth, n_out // tn),
        in_specs=[pl.BlockSpec((B, D, LANES), lambda i, j: (0, 0, 0)),
                  pl.BlockSpec((None, D, tn), lambda i, j: (i, 0, j)),
                  pl.BlockSpec((None, 1, tn), lambda i, j: (i, 0, j))],
        out_specs=pl.BlockSpec((None, B, tn), lambda i, j: (i, 0, j)),
        out_shape=jax.ShapeDtypeStruct((depth, B, n_out), F32),
        compiler_params=_cparams(("arbitrary", "arbitrary")),
        name="adaln",
    )(ct, ada_w, ada_b.reshape(depth, 1, n_out))


def _mod_kernel(x_ref, sc_ref, sh_ref, h_ref):
    h_ref[...] = (x_ref[...] * (1.0 + sc_ref[...]) + sh_ref[...]).astype(h_ref.dtype)


def modulate(x2, sc, sh, T, h_dtype):
    N, D = x2.shape
    tm = 512
    vec = pl.BlockSpec((None, 1, D), lambda i: ((i * tm) // T, 0, 0))
    return pl.pallas_call(
        _mod_kernel,
        grid=(N // tm,),
        in_specs=[pl.BlockSpec((tm, D), lambda i: (i, 0)), vec, vec],
        out_specs=pl.BlockSpec((tm, D), lambda i: (i, 0)),
        out_shape=jax.ShapeDtypeStruct((N, D), h_dtype),
        compiler_params=_cparams(("arbitrary",)),
        name="modulate",
    )(x2, sc, sh)


def _deepnorm(x, y, gate, lg, lb):
    z = ALPHA * x + (1.0 + gate) * y
    mu = jnp.mean(z, axis=-1, keepdims=True)
    zc = z - mu
    var = jnp.mean(zc * zc, axis=-1, keepdims=True)
    return zc * lax.rsqrt(var + LN_EPS) * lg + lb


def _norm_mod_kernel(x_ref, y_ref, g_ref, lg_ref, lb_ref, sc_ref, sh_ref, xo_ref, ho_ref):
    xn = _deepnorm(x_ref[...], y_ref[...].astype(F32), g_ref[...], lg_ref[...], lb_ref[...])
    xo_ref[...] = xn
    ho_ref[...] = (xn * (1.0 + sc_ref[...]) + sh_ref[...]).astype(ho_ref.dtype)


def _norm_kernel(x_ref, y_ref, g_ref, lg_ref, lb_ref, xo_ref):
    xo_ref[...] = _deepnorm(x_ref[...], y_ref[...].astype(F32), g_ref[...], lg_ref[...], lb_ref[...])


def norm_mod(x2, y, gate, lg, lb, sc, sh, T, h_dtype):
    N, D = x2.shape
    tm = 256
    row = pl.BlockSpec((tm, D), lambda i: (i, 0))
    vec = pl.BlockSpec((None, 1, D), lambda i: ((i * tm) // T, 0, 0))
    par = pl.BlockSpec((1, D), lambda i: (0, 0))
    if sc is None:
        return pl.pallas_call(
            _norm_kernel, grid=(N // tm,),
            in_specs=[row, row, vec, par, par], out_specs=row,
            out_shape=jax.ShapeDtypeStruct((N, D), F32),
            compiler_params=_cparams(("arbitrary",)), name="deepnorm",
        )(x2, y, gate, lg, lb), None
    return pl.pallas_call(
        _norm_mod_kernel, grid=(N // tm,),
        in_specs=[row, row, vec, par, par, vec, vec], out_specs=[row, row],
        out_shape=[jax.ShapeDtypeStruct((N, D), F32), jax.ShapeDtypeStruct((N, D), h_dtype)],
        compiler_params=_cparams(("arbitrary",)), name="deepnorm_mod",
    )(x2, y, gate, lg, lb, sc, sh)


def _mm_kernel(x_ref, w_ref, o_ref, *, scale):
    acc = _dot(x_ref[...], w_ref[...])
    if scale != 1.0:
        acc = acc * scale
    o_ref[...] = acc.astype(o_ref.dtype)


def matmul(x, w, out_dtype, *, tm=1024, tn=512, scale=1.0, name="matmul"):
    M, K = x.shape
    N = w.shape[1]
    tn = min(tn, N)
    return pl.pallas_call(
        functools.partial(_mm_kernel, scale=scale),
        grid=(M // tm, N // tn),
        in_specs=[pl.BlockSpec((tm, K), lambda i, j: (i, 0)),
                  pl.BlockSpec((K, tn), lambda i, j: (0, j))],
        out_specs=pl.BlockSpec((tm, tn), lambda i, j: (i, j)),
        out_shape=jax.ShapeDtypeStruct((M, N), out_dtype),
        compiler_params=_cparams(("arbitrary", "arbitrary")),
        name=name,
    )(x, w)


def _mm_glu_kernel(x_ref, wa_ref, wb_ref, o_ref):
    x = x_ref[...]
    a = _dot(x, wa_ref[...])
    b = _dot(x, wb_ref[...])
    o_ref[...] = (a * _sigmoid(b)).astype(o_ref.dtype)


def matmul_glu(x, w, out_dtype, *, tm=1024, tn=512):
    M, K = x.shape
    n = w.shape[1] // 2
    nj = n // tn
    return pl.pallas_call(
        _mm_glu_kernel,
        grid=(M // tm, nj),
        in_specs=[pl.BlockSpec((tm, K), lambda i, j: (i, 0)),
                  pl.BlockSpec((K, tn), lambda i, j: (0, j)),
                  pl.BlockSpec((K, tn), lambda i, j: (0, j + nj))],
        out_specs=pl.BlockSpec((tm, tn), lambda i, j: (i, j)),
        out_shape=jax.ShapeDtypeStruct((M, n), out_dtype),
        compiler_params=_cparams(("arbitrary", "arbitrary")),
        name="matmul_glu",
    )(x, w, w)


def _ffn_kernel(x_ref, w1_ref, w3_ref, w2_ref, o_ref):
    f = pl.program_id(1)
    x = x_ref[...]
    a = _dot(x, w1_ref[...])
    b = _dot(x, w3_ref[...])
    part = _dot((_silu(a) * b).astype(BF16), w2_ref[...])

    @pl.when(f == 0)
    def _():
        o_ref[...] = part

    @pl.when(f > 0)
    def _():
        o_ref[...] += part


def ffn_swiglu(h, w1, w3, w2, *, tm=512, tf=512):
    N, D = h.shape
    Fd = w1.shape[1]
    return pl.pallas_call(
        _ffn_kernel,
        grid=(N // tm, Fd // tf),
        in_specs=[pl.BlockSpec((tm, D), lambda i, f: (i, 0)),
                  pl.BlockSpec((D, tf), lambda i, f: (0, f)),
                  pl.BlockSpec((D, tf), lambda i, f: (0, f)),
                  pl.BlockSpec((tf, D), lambda i, f: (f, 0))],
        out_specs=pl.BlockSpec((tm, D), lambda i, f: (i, 0)),
        out_shape=jax.ShapeDtypeStruct((N, D), F32),
        compiler_params=_cparams(("arbitrary", "arbitrary")),
        name="ffn_swiglu",
    )(h, w1, w3, w2)


def _router_kernel(h_ref, rh_ref, rl_ref, idx_ref, wt_ref):
    h = h_ref[...]
    hh = h.astype(BF16)
    hl = (h - hh.astype(F32)).astype(BF16)
    rh = rh_ref[...]
    logits = _dot(hh, rh) + _dot(hl, rh) + _dot(hh, rl_ref[...])
    lane = lax.broadcasted_iota(jnp.int32, logits.shape, 1)
    lanef = lane.astype(F32)
    logits = jnp.where(lane < N_EXPERTS, logits, -jnp.inf)
    m1 = jnp.max(logits, axis=1, keepdims=True)
    i1 = jnp.min(jnp.where(logits == m1, lanef, float(LANES)), axis=1, keepdims=True)
    rest = jnp.where(lanef == i1, -jnp.inf, logits)
    m2 = jnp.max(rest, axis=1, keepdims=True)
    i2 = jnp.min(jnp.where(rest == m2, lanef, float(LANES)), axis=1, keepdims=True)
    e2 = jnp.exp(m2 - m1)
    w1 = 1.0 / (1.0 + e2)
    w2 = e2 / (1.0 + e2)
    idx_ref[...] = jnp.where(lane == 0, i1, jnp.where(lane == 1, i2, 0.0)).astype(jnp.int32)
    wt_ref[...] = jnp.where(lane == 0, w1, jnp.where(lane == 1, w2, 0.0))


def moe_router(h, router):
    N, D = h.shape
    tm = 512
    rp = jnp.zeros((D, LANES), F32).at[:, :N_EXPERTS].set(router)
    rh = rp.astype(BF16)
    rl = (rp - rh.astype(F32)).astype(BF16)
    row = pl.BlockSpec((tm, LANES), lambda i: (i, 0))
    return pl.pallas_call(
        _router_kernel, grid=(N // tm,),
        in_specs=[pl.BlockSpec((tm, D), lambda i: (i, 0)),
                  pl.BlockSpec((D, LANES), lambda i: (0, 0)),
                  pl.BlockSpec((D, LANES), lambda i: (0, 0))],
        out_specs=[row, row],
        out_shape=[jax.ShapeDtypeStruct((N, LANES), jnp.int32), jax.ShapeDtypeStruct((N, LANES), F32)],
        compiler_params=_cparams(("arbitrary",)), name="moe_router",
    )(h, rh, rl)


def _row_copy(src_hbm, dst_ref, sem, src_row, dst_row):
    return pltpu.make_async_copy(src_hbm.at[pl.ds(src_row, 1), :], dst_ref.at[pl.ds(dst_row, 1), :], sem)


def _gather_rows_kernel(tok_ref, x_hbm, o_ref, sem, *, rows):
    base = pl.program_id(0) * rows

    def start(r, carry):
        _row_copy(x_hbm, o_ref, sem, tok_ref[base + r], r).start()
        return carry

    def wait(r, carry):
        _row_copy(x_hbm, o_ref, sem, tok_ref[base + r], r).wait()
        return carry

    lax.fori_loop(0, rows, start, 0)
    lax.fori_loop(0, rows, wait, 0)


def gather_rows(x, row_tok, *, rows=256):
    n_out = row_tok.shape[0]
    D = x.shape[1]
    return pl.pallas_call(
        functools.partial(_gather_rows_kernel, rows=rows),
        grid_spec=pltpu.PrefetchScalarGridSpec(
            num_scalar_prefetch=1, grid=(n_out // rows,),
            in_specs=[pl.BlockSpec(memory_space=pl.ANY)],
            out_specs=pl.BlockSpec((rows, D), lambda i, tok: (i, 0)),
            scratch_shapes=[pltpu.SemaphoreType.DMA(())]),
        out_shape=jax.ShapeDtypeStruct((n_out, D), x.dtype),
        compiler_params=_cparams(("arbitrary",)), name="moe_gather",
    )(row_tok, x)


def _moe_mm_kernel(te_ref, tv_ref, x_ref, w1_ref, w3_ref, w2_ref, o_ref, xb_ref):
    i = pl.program_id(0)
    f = pl.program_id(1)

    @pl.when(f == 0)
    def _():
        xb_ref[...] = x_ref[...].astype(BF16)
        o_ref[...] = jnp.zeros_like(o_ref)

    @pl.when(tv_ref[i] > 0)
    def _():
        x = xb_ref[...]
        a = _dot(x, w1_ref[...])
        b = _dot(x, w3_ref[...])
        o_ref[...] += _dot((_silu(a) * b).astype(BF16), w2_ref[...])


def moe_grouped_swiglu(xs, tile_expert, tile_valid, w1, w3, w2, *, tm, tf=512):
    R, D = xs.shape
    Fd = w1.shape[2]
    return pl.pallas_call(
        _moe_mm_kernel,
        grid_spec=pltpu.PrefetchScalarGridSpec(
            num_scalar_prefetch=2, grid=(R // tm, Fd // tf),
            in_specs=[pl.BlockSpec((tm, D), lambda i, f, te, tv: (i, 0)),
                      pl.BlockSpec((None, D, tf), lambda i, f, te, tv: (te[i], 0, f * tv[i])),
                      pl.BlockSpec((None, D, tf), lambda i, f, te, tv: (te[i], 0, f * tv[i])),
                      pl.BlockSpec((None, tf, D), lambda i, f, te, tv: (te[i], f * tv[i], 0))],
            out_specs=pl.BlockSpec((tm, D), lambda i, f, te, tv: (i, 0)),
            scratch_shapes=[pltpu.VMEM((tm, D), BF16)]),
        out_shape=jax.ShapeDtypeStruct((R, D), F32),
        compiler_params=_cparams(("arbitrary", "arbitrary")), name="moe_grouped_swiglu",
    )(tile_expert, tile_valid, xs, w1, w3, w2)


def _combine_gather(pos_ref, ys_hbm, buf_ref, sem, base, rows):
    def start(r, carry):
        _row_copy(ys_hbm, buf_ref.at[0], sem, pos_ref[2 * (base + r)], r).start()
        _row_copy(ys_hbm, buf_ref.at[1], sem, pos_ref[2 * (base + r) + 1], r).start()
        return carry

    def wait(r, carry):
        _row_copy(ys_hbm, buf_ref.at[0], sem, pos_ref[2 * (base + r)], r).wait()
        _row_copy(ys_hbm, buf_ref.at[1], sem, pos_ref[2 * (base + r) + 1], r).wait()
        return carry

    lax.fori_loop(0, rows, start, 0)
    lax.fori_loop(0, rows, wait, 0)


def _moe_combine(wt_ref, buf_ref):
    wt = wt_ref[...]
    return wt[:, 0:1] * buf_ref[0] + wt[:, 1:2] * buf_ref[1]


def _combine_norm_mod_kernel(pos_ref, ys_hbm, wt_ref, x_ref, g_ref, lg_ref, lb_ref, sc_ref, sh_ref,
                             xo_ref, ho_ref, buf_ref, sem, *, rows):
    _combine_gather(pos_ref, ys_hbm, buf_ref, sem, pl.program_id(0) * rows, rows)
    xn = _deepnorm(x_ref[...], _moe_combine(wt_ref, buf_ref), g_ref[...], lg_ref[...], lb_ref[...])
    xo_ref[...] = xn
    ho_ref[...] = (xn * (1.0 + sc_ref[...]) + sh_ref[...]).astype(ho_ref.dtype)


def _combine_norm_kernel(pos_ref, ys_hbm, wt_ref, x_ref, g_ref, lg_ref, lb_ref, xo_ref, buf_ref, sem, *, rows):
    _combine_gather(pos_ref, ys_hbm, buf_ref, sem, pl.program_id(0) * rows, rows)
    xo_ref[...] = _deepnorm(x_ref[...], _moe_combine(wt_ref, buf_ref), g_ref[...], lg_ref[...], lb_ref[...])


def moe_combine_norm_mod(ys, pos, wts, x2, gate, lg, lb, sc, sh, T, h_dtype, *, rows=256):
    N, D = x2.shape
    row = pl.BlockSpec((rows, D), lambda i, p: (i, 0))
    vec = pl.BlockSpec((None, 1, D), lambda i, p: ((i * rows) // T, 0, 0))
    par = pl.BlockSpec((1, D), lambda i, p: (0, 0))
    wsp = pl.BlockSpec((rows, LANES), lambda i, p: (i, 0))
    anysp = pl.BlockSpec(memory_space=pl.ANY)
    scratch = [pltpu.VMEM((2, rows, D), F32), pltpu.SemaphoreType.DMA(())]
    if sc is None:
        out = pl.pallas_call(
            functools.partial(_combine_norm_kernel, rows=rows),
            grid_spec=pltpu.PrefetchScalarGridSpec(
                num_scalar_prefetch=1, grid=(N // rows,),
                in_specs=[anysp, wsp, row, vec, par, par], out_specs=row, scratch_shapes=scratch),
            out_shape=jax.ShapeDtypeStruct((N, D), F32),
            compiler_params=_cparams(("arbitrary",)), name="moe_combine_norm",
        )(pos, ys, wts, x2, gate, lg, lb)
        return out, None
    return pl.pallas_call(
        functools.partial(_combine_norm_mod_kernel, rows=rows),
        grid_spec=pltpu.PrefetchScalarGridSpec(
            num_scalar_prefetch=1, grid=(N // rows,),
            in_specs=[anysp, wsp, row, vec, par, par, vec, vec], out_specs=[row, row],
            scratch_shapes=scratch),
        out_shape=[jax.ShapeDtypeStruct((N, D), F32), jax.ShapeDtypeStruct((N, D), h_dtype)],
        compiler_params=_cparams(("arbitrary",)), name="moe_combine_norm_mod",
    )(pos, ys, wts, x2, gate, lg, lb, sc, sh)


def _route_tables(idx2, tm, n_tiles):
    n_slots = idx2.size
    e_flat = idx2.reshape(-1)
    order = jnp.argsort(e_flat, stable=True).astype(jnp.int32)
    counts = jnp.sum((e_flat[:, None] == jnp.arange(N_EXPERTS)[None, :]).astype(jnp.int32), axis=0)
    tiles_per = (counts + tm - 1) // tm
    tile_end = jnp.cumsum(tiles_per)
    pad_start = (tile_end - tiles_per) * tm
    grp_start = jnp.cumsum(counts) - counts
    t = jnp.arange(n_tiles)
    te = jnp.sum((t[:, None] >= tile_end[None, :]).astype(jnp.int32), axis=1)
    tile_valid = (te < N_EXPERTS).astype(jnp.int32)
    tile_expert = jnp.minimum(te, N_EXPERTS - 1).astype(jnp.int32)
    r = jnp.arange(n_tiles * tm)
    e_r = tile_expert[r // tm]
    within = r - pad_start[e_r]
    ok = (within < counts[e_r]) & (tile_valid[r // tm] > 0)
    p = jnp.clip(within + grp_start[e_r], 0, n_slots - 1)
    row_tok = jnp.where(ok, order[p] // TOP_K, 0).astype(jnp.int32)
    inv = jnp.zeros((n_slots,), jnp.int32).at[order].set(jnp.arange(n_slots, dtype=jnp.int32))
    pos = (inv - grp_start[e_flat] + pad_start[e_flat]).astype(jnp.int32)
    return tile_expert, tile_valid, row_tok, pos


def moe_sublayer(h, x2, router, w1, w3, w2, gate, lg, lb, sc, sh, T, h_dtype, *, tm=512):
    N, D = h.shape
    idx, wts = moe_router(h, router)
    n_tiles = (N * TOP_K) // tm + N_EXPERTS
    tile_expert, tile_valid, row_tok, pos = _route_tables(idx[:, :TOP_K], tm, n_tiles)
    xs = gather_rows(h, row_tok)
    ys = moe_grouped_swiglu(xs, tile_expert, tile_valid, w1, w3, w2, tm=tm)
    return moe_combine_norm_mod(ys, pos, wts, x2, gate, lg, lb, sc, sh, T, h_dtype)


DW_HALO = 32
DW_ROWS = 128


def _dwconv_kernel(u_ref, halo_ref, dw_ref, lg_ref, lb_ref, o_ref, ext_ref, *, tiles_per_seq):
    i = pl.program_id(0)
    first = (i % tiles_per_seq) == 0
    ext_ref[0:DW_HALO, :] = jnp.where(first, 0.0, halo_ref[...].astype(F32))
    ext_ref[DW_HALO:, :] = u_ref[...].astype(F32)
    lead = DW_HALO - (CONV_WIDTH - 1)
    lg = lg_ref[...]
    lb = lb_ref[...]
    for rc in range(DW_ROWS // SUBLANES):
        r0 = rc * SUBLANES
        acc = None
        for k in range(CONV_WIDTH):
            term = ext_ref[r0 + lead + k:r0 + lead + k + SUBLANES, :] * dw_ref[k:k + 1, :]
            acc = term if acc is None else acc + term
        mu = jnp.mean(acc, axis=-1, keepdims=True)
        zc = acc - mu
        var = jnp.mean(zc * zc, axis=-1, keepdims=True)
        v = zc * lax.rsqrt(var + LN_EPS) * lg + lb
        o_ref[r0:r0 + SUBLANES, :] = _silu(v).astype(o_ref.dtype)


def dwconv_ln_silu(u, dw, lg, lb, T):
    N, D = u.shape
    hb = DW_ROWS // DW_HALO
    return pl.pallas_call(
        functools.partial(_dwconv_kernel, tiles_per_seq=T // DW_ROWS),
        grid=(N // DW_ROWS,),
        in_specs=[pl.BlockSpec((DW_ROWS, D), lambda i: (i, 0)),
                  pl.BlockSpec((DW_HALO, D), lambda i: (jnp.maximum(i * hb - 1, 0), 0)),
                  pl.BlockSpec((CONV_WIDTH, D), lambda i: (0, 0)),
                  pl.BlockSpec((1, D), lambda i: (0, 0)),
                  pl.BlockSpec((1, D), lambda i: (0, 0))],
        out_specs=pl.BlockSpec((DW_ROWS, D), lambda i: (i, 0)),
        out_shape=jax.ShapeDtypeStruct((N, D), BF16),
        scratch_shapes=[pltpu.VMEM((DW_HALO + DW_ROWS, D), F32)],
        compiler_params=_cparams(("arbitrary",)), name="dwconv_ln_silu",
    )(u, u, dw, lg.reshape(1, D), lb.reshape(1, D))


def conformer_mixer(h, w_in, dw, lg, lb, w_out, T):
    u = matmul_glu(h, w_in.astype(BF16), BF16)
    v = dwconv_ln_silu(u, dw, lg, lb, T)
    return matmul(v, w_out.astype(BF16), F32, name="conv_out")


GC_HALO = 16
GC_ROWS = 512
GC_COLS = 1024


def _gconv_kernel(x_ref, halo_ref, w_ref, o_ref, ext_ref, *, tiles_per_seq):
    i = pl.program_id(0)
    first = (i % tiles_per_seq) == 0
    ext_ref[0:GC_HALO, :] = jnp.where(first, 0.0, halo_ref[...].astype(F32))
    ext_ref[GC_HALO:, :] = x_ref[...].astype(F32)
    lead = GC_HALO - (GDN_CONV - 1)
    acc = None
    for k in range(GDN_CONV):
        term = ext_ref[lead + k:lead + k + GC_ROWS, :] * w_ref[k:k + 1, :]
        acc = term if acc is None else acc + term
    o_ref[...] = _silu(acc).astype(o_ref.dtype)


def gdn_conv_silu(x, w, T):
    N, C = x.shape
    hb = GC_ROWS // GC_HALO
    return pl.pallas_call(
        functools.partial(_gconv_kernel, tiles_per_seq=T // GC_ROWS),
        grid=(N // GC_ROWS, C // GC_COLS),
        in_specs=[pl.BlockSpec((GC_ROWS, GC_COLS), lambda i, j: (i, j)),
                  pl.BlockSpec((GC_HALO, GC_COLS), lambda i, j: (jnp.maximum(i * hb - 1, 0), j)),
                  pl.BlockSpec((GDN_CONV, GC_COLS), lambda i, j: (0, j))],
        out_specs=pl.BlockSpec((GC_ROWS, GC_COLS), lambda i, j: (i, j)),
        out_shape=jax.ShapeDtypeStruct((N, C), BF16),
        scratch_shapes=[pltpu.VMEM((GC_HALO + GC_ROWS, GC_COLS), F32)],
        compiler_params=_cparams(("arbitrary", "arbitrary")), name="gdn_conv_silu",
    )(x, x, w)


GP_ROWS = 512
GP_SUB = 256


def _l2norm(t):
    return t * lax.rsqrt(jnp.sum(t * t, axis=-1, keepdims=True) + 1e-6)


def _gdn_prep_kernel(q_ref, k_ref, v_ref, braw_ref, araw_ref, alog_ref, dtb_ref,
                     u_ref, w_ref, qg_ref, kdt_ref, attn_ref, gl_ref):
    C = GDN_CHUNK
    S = GP_SUB
    ri = lax.broadcasted_iota(jnp.int32, (S, S), 0)
    ci = lax.broadcasted_iota(jnp.int32, (S, S), 1)
    same = _shr(ri, C) == _shr(ci, C)
    lower = same & (ri >= ci)
    strict = same & (ri > ci)
    eye = ri == ci
    cum_mat = (same & (ri <= ci)).astype(BF16)
    tot_mat = same.astype(BF16)
    eye_f = eye.astype(F32)

    def to_col(row):
        return jnp.sum(jnp.where(eye, jnp.broadcast_to(row, (S, S)), 0.0), axis=1, keepdims=True)

    for sb in range(GP_ROWS // S):
        rows = slice(sb * S, (sb + 1) * S)
        q = _l2norm(q_ref[rows, :].astype(F32)) * (HEAD_DIM ** -0.5)
        k = _l2norm(k_ref[rows, :].astype(F32))
        kb16 = k.astype(BF16)
        gram = _dot_nt(kb16, kb16)
        qk = _dot_nt(q.astype(BF16), kb16)
        for e in range(2):
            a_row = araw_ref[e, :, rows]
            b_row = braw_ref[e, :, rows]
            g_row = -jnp.exp(alog_ref[e, :, 0:1]) * _softplus(a_row + dtb_ref[e, :, 0:1])
            beta_row = _sigmoid(b_row)
            g8 = jnp.broadcast_to(g_row, (SUBLANES, S))
            gcum_row = _dot_x3(g8, cum_mat)[0:1, :]
            gtot_row = _dot_x3(g8, tot_mat)[0:1, :]
            gcum = to_col(gcum_row)
            gtot = to_col(gtot_row)
            beta = to_col(beta_row)
            decay = jnp.exp(jnp.where(lower, gcum - gcum_row, NEG_INF))
            a = jnp.where(strict, beta * gram * decay, 0.0)
            inv = eye_f - a
            pw = a
            for _ in range(int(math.log2(C)) - 1):
                pw16 = pw.astype(BF16)
                pw = _dot(pw16, pw16)
                inv = inv + _dot(inv.astype(BF16), pw.astype(BF16))
            eg = jnp.exp(gcum)
            v = v_ref[rows, e * HEAD_DIM:(e + 1) * HEAD_DIM].astype(F32)
            inv16 = inv.astype(BF16)
            u = _dot(inv16, (v * beta).astype(BF16))
            w = _dot(inv16, (k * (beta * eg)).astype(BF16))
            cols = slice(e * HEAD_DIM, (e + 1) * HEAD_DIM)
            u_ref[rows, cols] = u
            w_ref[rows, cols] = w.astype(BF16)
            qg_ref[rows, cols] = (q * eg).astype(BF16)
            kdec = k * jnp.exp(gtot - gcum)
            kdt_ref[e, :, rows] = kdec.T.astype(BF16)
            attn = qk * decay
            for c in range(S // C):
                attn_ref[e, sb * S + c * C:sb * S + (c + 1) * C, :] = (
                    attn[c * C:(c + 1) * C, c * C:(c + 1) * C].astype(BF16))
                gl_ref[e, sb * (S // C) + c:sb * (S // C) + c + 1, :] = jnp.broadcast_to(
                    jnp.exp(gtot[c * C:c * C + 1, :]), (1, LANES))


def gdn_prep(qkv, baT, a_log, dt_bias, B, T):
    N = qkv.shape[0]
    HK, HV, dh = GDN_QK_HEADS, GDN_V_HEADS, HEAD_DIM
    nt = T // GP_ROWS
    nc = GP_ROWS // GDN_CHUNK
    rowmap = lambda b, hk, t: (b * nt + t)
    alog = jnp.broadcast_to(a_log.astype(F32)[:, None, None], (HV, 1, LANES))
    dtb = jnp.broadcast_to(dt_bias.astype(F32)[:, None, None], (HV, 1, LANES))
    out_shape = [jax.ShapeDtypeStruct((N, HV * dh), F32),
                 jax.ShapeDtypeStruct((N, HV * dh), BF16),
                 jax.ShapeDtypeStruct((N, HV * dh), BF16),
                 jax.ShapeDtypeStruct((B, HV, dh, T), BF16),
                 jax.ShapeDtypeStruct((B, HV, T, GDN_CHUNK), BF16),
                 jax.ShapeDtypeStruct((B, HV, T // GDN_CHUNK, LANES), F32)]
    big = pl.BlockSpec((GP_ROWS, 2 * dh), lambda b, hk, t: (rowmap(b, hk, t), hk))
    return pl.pallas_call(
        _gdn_prep_kernel,
        grid=(B, HK, nt),
        in_specs=[pl.BlockSpec((GP_ROWS, dh), lambda b, hk, t: (rowmap(b, hk, t), hk)),
                  pl.BlockSpec((GP_ROWS, dh), lambda b, hk, t: (rowmap(b, hk, t), HK + hk)),
                  pl.BlockSpec((GP_ROWS, 2 * dh), lambda b, hk, t: (rowmap(b, hk, t), HK + hk)),
                  pl.BlockSpec((None, 2, 1, GP_ROWS), lambda b, hk, t: (b, hk, 0, t)),
                  pl.BlockSpec((None, 2, 1, GP_ROWS), lambda b, hk, t: (b, HK + hk, 0, t)),
                  pl.BlockSpec((2, 1, LANES), lambda b, hk, t: (hk, 0, 0)),
                  pl.BlockSpec((2, 1, LANES), lambda b, hk, t: (hk, 0, 0))],
        out_specs=[big, big, big,
                   pl.BlockSpec((None, 2, dh, GP_ROWS), lambda b, hk, t: (b, hk, 0, t)),
                   pl.BlockSpec((None, 2, GP_ROWS, GDN_CHUNK), lambda b, hk, t: (b, hk, t, 0)),
                   pl.BlockSpec((None, 2, nc, LANES), lambda b, hk, t: (b, hk, t, 0))],
        out_shape=out_shape,
        compiler_params=_cparams(("arbitrary", "arbitrary", "arbitrary")), name="gdn_prep",
    )(qkv, qkv, qkv, baT, baT, alog, dtb)


GS_HEADS = 4
GS_ROWS = 512


def _gdn_scan_kernel(u_ref, w_ref, qg_ref, kdt_ref, attn_ref, gl_ref, z_ref, nw_ref, o_ref, s_ref):
    C = GDN_CHUNK

    @pl.when(pl.program_id(2) == 0)
    def _():
        s_ref[...] = jnp.zeros_like(s_ref)

    nw = nw_ref[...]
    for c in range(GS_ROWS // C):
        rows = slice(c * C, (c + 1) * C)
        for hh in range(GS_HEADS):
            cols = slice(hh * HEAD_DIM, (hh + 1) * HEAD_DIM)
            s = s_ref[hh]
            lhs = jnp.concatenate([w_ref[rows, cols], qg_ref[rows, cols]], axis=0)
            r = _dot(lhs, s.astype(BF16))
            v_new = u_ref[rows, cols] - r[:C]
            lhs2 = jnp.concatenate([attn_ref[hh, rows, :], kdt_ref[hh, :, rows]], axis=0)
            r2 = _dot(lhs2, v_new.astype(BF16))
            o = r[C:] + r2[:C]
            s_ref[hh] = s * gl_ref[hh, c:c + 1, :] + r2[C:]
            z = z_ref[rows, cols].astype(F32)
            o = o * lax.rsqrt(jnp.mean(o * o, axis=-1, keepdims=True) + 1e-6) * nw * _silu(z)
            o_ref[rows, cols] = o.astype(o_ref.dtype)


def gdn_scan(u, w, qg, kdt, attn, gl, z, norm_w, B, T):
    N = u.shape[0]
    HV, dh = GDN_V_HEADS, HEAD_DIM
    nt = T // GS_ROWS
    nc = GS_ROWS // GDN_CHUNK
    wide = pl.BlockSpec((GS_ROWS, GS_HEADS * dh), lambda b, g, t: (b * nt + t, g))
    return pl.pallas_call(
        _gdn_scan_kernel,
        grid=(B, HV // GS_HEADS, nt),
        in_specs=[wide, wide, wide,
                  pl.BlockSpec((None, GS_HEADS, dh, GS_ROWS), lambda b, g, t: (b, g, 0, t)),
                  pl.BlockSpec((None, GS_HEADS, GS_ROWS, GDN_CHUNK), lambda b, g, t: (b, g, t, 0)),
                  pl.BlockSpec((None, GS_HEADS, nc, LANES), lambda b, g, t: (b, g, t, 0)),
                  wide,
                  pl.BlockSpec((1, dh), lambda b, g, t: (0, 0))],
        out_specs=wide,
        out_shape=jax.ShapeDtypeStruct((N, HV * dh), BF16),
        scratch_shapes=[pltpu.VMEM((GS_HEADS, dh, dh), F32)],
        compiler_params=_cparams(("arbitrary", "arbitrary", "arbitrary")), name="gdn_scan",
    )(u, w, qg, kdt, attn, gl, z, norm_w.astype(F32).reshape(1, dh))


def gdn_mixer(h, w_in, conv_w, a_log, dt_bias, norm_w, w_out, B, T):
    N, D = h.shape
    qkv_dim = 2 * GDN_QK_HEADS * HEAD_DIM + GDN_V_HEADS * HEAD_DIM
    v_dim = GDN_V_HEADS * HEAD_DIM
    w_qkv = w_in[:, :qkv_dim].astype(BF16)
    w_z = w_in[:, qkv_dim:qkv_dim + v_dim].astype(BF16)
    w_ba = jnp.zeros((D, LANES), BF16).at[:, :2 * GDN_V_HEADS].set(w_in[:, qkv_dim + v_dim:].astype(BF16))
    qkv = matmul(h, w_qkv, BF16, name="gdn_in_qkv")
    z = matmul(h, w_z, BF16, name="gdn_in_z")
    ba = matmul(h, w_ba, F32, name="gdn_in_ba")
    qkv = gdn_conv_silu(qkv, conv_w, T)
    baT = ba[:, :2 * GDN_V_HEADS].reshape(B, T, 2 * GDN_V_HEADS).transpose(0, 2, 1)[:, :, None, :]
    u, w, qg, kdt, attn, gl = gdn_prep(qkv, baT, a_log, dt_bias, B, T)
    o = gdn_scan(u, w, qg, kdt, attn, gl, z, norm_w, B, T)
    return matmul(o, w_out.astype(BF16), F32, name="gdn_out")


def _rel_bucket(dist):
    n = jnp.maximum(dist, 0)
    large = REL_EXACT + (jnp.log(jnp.maximum(n, 1).astype(F32) / REL_EXACT)
                         / math.log(REL_MAX_DIST / REL_EXACT) * (REL_BUCKETS - REL_EXACT)).astype(jnp.int32)
    return jnp.where(n < REL_EXACT, n, jnp.minimum(large, REL_BUCKETS - 1))


N_TOEP = 9
BIAS_PAD = 144


def _toeplitz(rows_rev):
    lead = rows_rev.shape[:-1]
    n = Q_BLOCK
    t = jnp.tile(rows_rev, (1,) * len(lead) + (n,))[..., :n * 2 * n].reshape(lead + (n, 2 * n))
    return t[..., n:]


def _nsa_bias_tables(rel_bias, n_cmp):
    n_max = Q_BLOCK * N_TOEP + BIAS_PAD
    dist = jnp.arange(-BIAS_PAD, n_max + 1)
    ftab = rel_bias.astype(F32)[_rel_bucket(dist)].T
    far = rel_bias.astype(F32)[REL_BUCKETS - 1]

    def family(shift, n_d):
        rows = []
        for d in range(n_d):
            c = Q_BLOCK * d + Q_BLOCK + shift + BIAS_PAD
            rows.append(ftab[:, c - 2 * Q_BLOCK:c + 1][:, ::-1])
        return _toeplitz(jnp.stack(rows, axis=1))

    toep = family(0, N_TOEP)
    toep = toep.reshape(NSA_GROUPS, NSA_HPG, N_TOEP, Q_BLOCK, Q_BLOCK).transpose(0, 2, 1, 3, 4)
    toep = toep.reshape(NSA_GROUPS, N_TOEP, NSA_HPG * Q_BLOCK, Q_BLOCK)
    cm = family(-CMP_STRIDE, N_TOEP - 1)[..., ::CMP_STRIDE]
    strip = jnp.concatenate([cm[:, d] for d in range(N_TOEP - 2, -1, -1)], axis=-1)
    fill = jnp.broadcast_to(far[:, None, None], (NSA_HEADS, Q_BLOCK, n_cmp - strip.shape[-1]))
    cmpb = jnp.concatenate([strip, fill], axis=-1).reshape(NSA_GROUPS, NSA_HPG * Q_BLOCK, n_cmp)
    return toep, cmpb


def _compress_kernel(x_ref, pos_ref, w1a_ref, w1b_ref, w2_ref, o_ref):
    x = x_ref[...].astype(F32)
    pos = pos_ref[...]
    a = _dot((x + pos[0:1, :]).astype(BF16), w1a_ref[...])
    b = _dot((x + pos[1:2, :]).astype(BF16), w1b_ref[...])
    n = x.shape[0]
    hid = a + pltpu.roll(b, n - 1, axis=0)
    o_ref[...] = _dot(_silu(hid).astype(BF16), w2_ref[...]).astype(o_ref.dtype)


def nsa_compress(kv_cmp, cmp_pos, w1k, w2k, w1v, w2v, B, T):
    G, dh = NSA_GROUPS, HEAD_DIM
    half = CMP_BLOCK // 2
    nch = T // half
    x = kv_cmp.reshape(B, nch, half, 2, G, dh).transpose(3, 0, 4, 1, 2, 5).reshape(2, B, G, nch, half * dh)
    pos = cmp_pos.astype(F32).reshape(2, 2, half * dh)
    w1 = jnp.stack([w1k, w1v]).astype(BF16)
    w2 = jnp.stack([w2k, w2v]).astype(BF16)
    return pl.pallas_call(
        _compress_kernel,
        grid=(2, B, G),
        in_specs=[pl.BlockSpec((None, None, None, nch, half * dh), lambda s, b, g: (s, b, g, 0, 0)),
                  pl.BlockSpec((None, 2, half * dh), lambda s, b, g: (s, 0, 0)),
                  pl.BlockSpec((None, half * dh, CMP_HIDDEN), lambda s, b, g: (s, 0, 0)),
                  pl.BlockSpec((None, half * dh, CMP_HIDDEN), lambda s, b, g: (s, 1, 0)),
                  pl.BlockSpec((None, CMP_HIDDEN, dh), lambda s, b, g: (s, 0, 0))],
        out_specs=pl.BlockSpec((None, None, None, nch, dh), lambda s, b, g: (s, b, g, 0, 0)),
        out_shape=jax.ShapeDtypeStruct((2, B, G, nch, dh), BF16),
        compiler_params=_cparams(("arbitrary", "arbitrary", "arbitrary")), name="nsa_compress",
    )(x, pos, w1, w1, w2)


SLC_TILE = 256
WIN_KEYS = WINDOW + Q_BLOCK


def _nsa_attn_kernel(q_ref, kc_ref, vc_ref, ks_ref, vs_ref, kw_ref, vw_ref, gate_ref, toep_ref, cmpb_ref,
                     ovl_ref, o_ref, acc_ref, m_ref, l_ref):
    qi = pl.program_id(2)
    qs = qi * Q_BLOCK
    HPG, dh, QB = NSA_HPG, HEAD_DIM, Q_BLOCK
    R = HPG * QB
    q = q_ref[...]
    Q = jnp.concatenate([q[:, h * dh:(h + 1) * dh] for h in range(HPG)], axis=0)
    tq_r = qs + (lax.broadcasted_iota(jnp.int32, (R, 1), 0) & (QB - 1))
    tq = qs + lax.broadcasted_iota(jnp.int32, (QB, 1), 0)

    n_cmp = kc_ref.shape[0]
    s = _dot_nt(Q, kc_ref[...])
    shift = lax.rem(8 * (qi - 7) + n_cmp, n_cmp)
    bias = pltpu.roll(cmpb_ref[...], shift, axis=1)
    cidx = lax.broadcasted_iota(jnp.int32, (1, n_cmp), 1)
    valid = (cidx * CMP_STRIDE + (CMP_BLOCK - 1)) <= tq_r
    s = jnp.where(valid, s + bias, NEG_INF)
    m = jnp.max(s, axis=1, keepdims=True)
    p = jnp.where(valid, jnp.exp(s - m), 0.0)
    l = jnp.sum(p, axis=1, keepdims=True)
    pn = p * jnp.where(l > 0.0, 1.0 / jnp.where(l > 0.0, l, 1.0), 0.0)
    o_cmp = _dot(pn.astype(BF16), vc_ref[...])

    p4 = pn[0:QB]
    for h in range(1, HPG):
        p4 = p4 + pn[h * QB:(h + 1) * QB]
    imp = _dot_x3(p4, ovl_ref[...])
    n_slc = imp.shape[1]
    blk = lax.broadcasted_iota(jnp.int32, (QB, n_slc), 1)
    blkf = blk.astype(F32)
    cur = _shr(tq, SLC_BLOCK)
    forced = (blk == 0) | (blk == cur) | (blk == cur - 1)
    work = jnp.where(forced, FORCE_SCORE, jnp.where(blk <= cur, imp, -1.0))
    sel = jnp.zeros((QB, n_slc), F32)
    for _ in range(min(SLC_TOP_N, n_slc)):
        mx = jnp.max(work, axis=1, keepdims=True)
        first = jnp.min(jnp.where(work == mx, blkf, float(n_slc)), axis=1, keepdims=True)
        hit = blkf == first
        sel = jnp.where(hit, 1.0, sel)
        work = jnp.where(hit, -2.0, work)
    sel16 = sel.astype(BF16)

    m_ref[...] = jnp.full(m_ref.shape, NEG_INF, F32)
    l_ref[...] = jnp.zeros(l_ref.shape, F32)
    acc_ref[...] = jnp.zeros(acc_ref.shape, F32)
    per_tile = SLC_TILE // SLC_BLOCK
    halves = SLC_TILE // QB

    def slc_step(j, carry):
        k0 = pl.multiple_of(j * SLC_TILE, SLC_TILE)
        kj = ks_ref[pl.ds(k0, SLC_TILE), :]
        vj = vs_ref[pl.ds(k0, SLC_TILE), :]
        sj = _dot_nt(Q, kj)
        d0 = qi - halves * j
        bias_j = jnp.concatenate(
            [toep_ref[jnp.clip(d0 - t, 0, N_TOEP - 1)] for t in range(halves)], axis=1)
        srow = lax.broadcasted_iota(jnp.int32, (n_slc, SLC_TILE), 0)
        kcol = lax.broadcasted_iota(jnp.int32, (n_slc, SLC_TILE), 1)
        expand = (srow == per_tile * j + _shr(kcol, SLC_BLOCK)).astype(BF16)
        selm = _dot(sel16, expand)
        kpos = k0 + lax.broadcasted_iota(jnp.int32, (1, SLC_TILE), 1)
        addm = jnp.where((selm > 0.5) & (kpos <= tq), 0.0, NEG_INF)
        sj = sj + bias_j + jnp.concatenate([addm] * HPG, axis=0)
        m_prev = m_ref[...]
        m_new = jnp.maximum(m_prev, jnp.max(sj, axis=1, keepdims=True))
        alpha = jnp.exp(m_prev - m_new)
        ps = [jnp.exp(sj[:, t * LANES:(t + 1) * LANES] - m_new) for t in range(SLC_TILE // LANES)]
        row_sum = ps[0]
        for t in range(1, len(ps)):
            row_sum = row_sum + ps[t]
        l_ref[...] = alpha * l_ref[...] + jnp.sum(row_sum, axis=1, keepdims=True)
        pcat = jnp.concatenate(ps, axis=1).astype(BF16)
        acc_ref[...] = alpha * acc_ref[...] + _dot(pcat, vj)
        m_ref[...] = m_new
        return carry

    lax.fori_loop(0, qi // halves + 1, slc_step, 0)
    o_slc = acc_ref[...] / l_ref[...]

    start = pl.multiple_of(jnp.maximum(qs - WINDOW, 0), QB)
    kwin = kw_ref[pl.ds(start, WIN_KEYS), :]
    vwin = vw_ref[pl.ds(start, WIN_KEYS), :]
    sw = _dot_nt(Q, kwin)
    doff = (qs - start) // QB
    parts = []
    for t in range(WIN_KEYS // QB):
        kpos = start + t * QB + lax.broadcasted_iota(jnp.int32, (1, QB), 1)
        dist = tq_r - kpos
        ok = (dist >= 0) & (dist < WINDOW)
        b_t = toep_ref[jnp.clip(doff - t, 0, N_TOEP - 1)]
        parts.append(jnp.where(ok, sw[:, t * QB:(t + 1) * QB] + b_t, NEG_INF))
    mw = parts[0]
    for t in range(1, len(parts)):
        mw = jnp.maximum(mw, parts[t])
    mw = jnp.max(mw, axis=1, keepdims=True)
    pw = [jnp.exp(pt - mw) for pt in parts]
    lw = pw[0]
    for t in range(1, len(pw)):
        lw = lw + pw[t]
    lw = jnp.sum(lw, axis=1, keepdims=True)
    o_win = _dot(jnp.concatenate(pw, axis=1).astype(BF16), vwin) / lw

    gt = _sigmoid(gate_ref[...])
    outs = []
    for h in range(HPG):
        rows = slice(h * QB, (h + 1) * QB)
        outs.append(gt[:, 3 * h:3 * h + 1] * o_cmp[rows] + gt[:, 3 * h + 1:3 * h + 2] * o_slc[rows]
                    + gt[:, 3 * h + 2:3 * h + 3] * o_win[rows])
    o_ref[...] = jnp.concatenate(outs, axis=1).astype(o_ref.dtype)


def nsa_attention(q, kv, kvc, gates, toep, cmpb, B, T):
    N = q.shape[0]
    G, HPG, dh, QB = NSA_GROUPS, NSA_HPG, HEAD_DIM, Q_BLOCK
    nq = T // QB
    n_cmp = T // CMP_STRIDE
    n_slc = T // SLC_BLOCK
    c = jnp.arange(n_cmp)[:, None] * CMP_STRIDE
    sblk = jnp.arange(n_slc)[None, :] * SLC_BLOCK
    ovl = ((c < sblk + SLC_BLOCK) & (c + CMP_BLOCK > sblk)).astype(BF16)

    def kv_spec(branch, which):
        col = (branch * 2 + which) * G
        return pl.BlockSpec((None, T, dh), lambda b, g, qi: (b, 0, col + g))

    def cmp_spec(which):
        return pl.BlockSpec((None, None, None, n_cmp, dh), lambda b, g, qi: (which, b, g, 0, 0))

    R = HPG * QB
    return pl.pallas_call(
        _nsa_attn_kernel,
        grid=(B, G, nq),
        in_specs=[pl.BlockSpec((QB, HPG * dh), lambda b, g, qi: (b * nq + qi, g)),
                  cmp_spec(0), cmp_spec(1),
                  kv_spec(1, 0), kv_spec(1, 1), kv_spec(2, 0), kv_spec(2, 1),
                  pl.BlockSpec((QB, LANES), lambda b, g, qi: (b * nq + qi, g)),
                  pl.BlockSpec((None, N_TOEP, R, QB), lambda b, g, qi: (g, 0, 0, 0)),
                  pl.BlockSpec((None, R, n_cmp), lambda b, g, qi: (g, 0, 0)),
                  pl.BlockSpec((n_cmp, n_slc), lambda b, g, qi: (0, 0))],
        out_specs=pl.BlockSpec((QB, HPG * dh), lambda b, g, qi: (b * nq + qi, g)),
        out_shape=jax.ShapeDtypeStruct((N, G * HPG * dh), BF16),
        scratch_shapes=[pltpu.VMEM((R, dh), F32), pltpu.VMEM((R, LANES), F32), pltpu.VMEM((R, LANES), F32)],
        compiler_params=_cparams(("arbitrary", "arbitrary", "arbitrary"), vmem_mib=56), name="nsa_attention",
    )(q, kvc, kvc, kv, kv, kv, kv, gates, toep, cmpb, ovl)


def nsa_mixer(h, w_in, cmp_pos, w1k, w2k, w1v, w2v, w_o, toep, cmpb, B, T):
    N, D = h.shape
    G, HPG, dh = NSA_GROUPS, NSA_HPG, HEAD_DIM
    q_dim = NSA_HEADS * dh
    kv_dim = 3 * 2 * G * dh
    wq = w_in[:, :q_dim].astype(BF16)
    wkv = w_in[:, q_dim:q_dim + kv_dim].astype(BF16)
    wg = w_in[:, q_dim + kv_dim:].astype(BF16).reshape(D, G, HPG * 3)
    wg = jnp.zeros((D, G, LANES), BF16).at[:, :, :HPG * 3].set(wg).reshape(D, G * LANES)
    q = matmul(h, wq, BF16, scale=dh ** -0.5, name="nsa_in_q")
    kv = matmul(h, wkv, BF16, name="nsa_in_kv")
    gates = matmul(h, wg, F32, name="nsa_in_gate")
    kv3 = kv.reshape(B, T, kv_dim)
    kvc = nsa_compress(kv3[:, :, :2 * G * dh], cmp_pos, w1k, w2k, w1v, w2v, B, T)
    o = nsa_attention(q, kv3, kvc, gates, toep, cmpb, B, T)
    return matmul(o, w_o.astype(BF16), F32, name="nsa_out")


def kernel(x, c, rel_bias, ada_w, ada_b, ln_g, ln_b, nsa_w_in, nsa_cmp_pos, nsa_cmp_w1k, nsa_cmp_w2k, nsa_cmp_w1v, nsa_cmp_w2v, nsa_w_o, conv_w_in, conv_dw, conv_ln_g, conv_ln_b, conv_w_out, gdn_w_in, gdn_conv, gdn_a_log, gdn_dt_bias, gdn_norm_w, gdn_w_out, ffn_w1, ffn_w3, ffn_w2, moe_router, moe_w1, moe_w3, moe_w2):
    B, T, D = x.shape
    N = B * T
    depth = ada_w.shape[0]
    mod = adaln_all(c, ada_w, ada_b).reshape(depth, B, 6, 1, D)
    toep, cmpb = _nsa_bias_tables(rel_bias, T // CMP_STRIDE)
    x2 = x.reshape(N, D)

    def h_dtype_for(layer, sub):
        return F32 if (sub == 1 and layer % 2 == 1) else BF16

    h = modulate(x2, mod[0, :, 1], mod[0, :, 0], T, h_dtype_for(0, 0))
    for i in range(depth):
        sh1, sc1, g1, sh2, sc2, g2 = (mod[i, :, k] for k in range(6))
        kind, j = i % 3, i // 3
        if kind == 0:
            y = nsa_mixer(h, nsa_w_in[j], nsa_cmp_pos[j], nsa_cmp_w1k[j], nsa_cmp_w2k[j], nsa_cmp_w1v[j],
                          nsa_cmp_w2v[j], nsa_w_o[j], toep, cmpb, B, T)
        elif kind == 1:
            y = conformer_mixer(h, conv_w_in[j], conv_dw[j], conv_ln_g[j], conv_ln_b[j], conv_w_out[j], T)
        else:
            y = gdn_mixer(h, gdn_w_in[j], gdn_conv[j], gdn_a_log[j], gdn_dt_bias[j], gdn_norm_w[j],
                          gdn_w_out[j], B, T)
        lg = ln_g[i].reshape(2, 1, D)
        lb = ln_b[i].reshape(2, 1, D)
        x2, h = norm_mod(x2, y, g1, lg[0], lb[0], sc2, sh2, T, h_dtype_for(i, 1))
        m = i // 2
        last = i == depth - 1
        sc_n = None if last else mod[i + 1, :, 1]
        sh_n = None if last else mod[i + 1, :, 0]
        if i % 2 == 0:
            y = ffn_swiglu(h, ffn_w1[m].astype(BF16), ffn_w3[m].astype(BF16), ffn_w2[m].astype(BF16))
            x2, h = norm_mod(x2, y, g2, lg[1], lb[1], sc_n, sh_n, T, h_dtype_for(i + 1, 0))
        else:
            x2, h = moe_sublayer(h, x2, moe_router[m], moe_w1[m].astype(BF16), moe_w3[m].astype(BF16),
                                 moe_w2[m].astype(BF16), g2, lg[1], lb[1], sc_n, sh_n, T, h_dtype_for(i + 1, 0))
    return x2.reshape(B, T, D)
```

```python
import functools
import math

import jax
import jax.numpy as jnp
from jax import lax
from jax.experimental import pallas as pl
from jax.experimental.pallas import tpu as pltpu

F32 = jnp.float32
BF16 = jnp.bfloat16

DEPTH = 4
ALPHA = (2 * DEPTH) ** 0.25
LN_EPS = 1e-5
NEG_INF = -1e30

REL_BUCKETS = 32
REL_EXACT = 16
REL_MAX_DIST = 1024

NSA_HEADS = 16
NSA_GROUPS = 4
NSA_HPG = NSA_HEADS // NSA_GROUPS
HEAD_DIM = 128
CMP_BLOCK = 32
CMP_STRIDE = 16
CMP_HIDDEN = 256
SLC_BLOCK = 64
SLC_TOP_N = 16
WINDOW = 512
Q_BLOCK = 128
FORCE_SCORE = 1e6

CONV_WIDTH = 31
GDN_QK_HEADS = 16
GDN_V_HEADS = 32
GDN_CONV = 4
GDN_CHUNK = 64

N_EXPERTS = 8
TOP_K = 2

LANES = 128
SUBLANES = 8
MIB = 1 << 20


def _cparams(sem, vmem_mib=48):
    return pltpu.CompilerParams(dimension_semantics=sem, vmem_limit_bytes=vmem_mib * MIB)


def _dot(a, b):
    return jnp.dot(a, b, preferred_element_type=F32)


def _dot_nt(a, b):
    return lax.dot_general(a, b, (((1,), (1,)), ((), ())), preferred_element_type=F32)


def _split3(a):
    hi = a.astype(BF16)
    r1 = a - hi.astype(F32)
    mid = r1.astype(BF16)
    lo = (r1 - mid.astype(F32)).astype(BF16)
    return hi, mid, lo


def _dot_x3(a, b_bf16):
    hi, mid, lo = _split3(a)
    return _dot(hi, b_bf16) + _dot(mid, b_bf16) + _dot(lo, b_bf16)


def _silu(x):
    return x * (1.0 / (1.0 + jnp.exp(-x)))


def _sigmoid(x):
    return 1.0 / (1.0 + jnp.exp(-x))


def _softplus(x):
    return jnp.maximum(x, 0.0) + jnp.log(1.0 + jnp.exp(-jnp.abs(x)))


def _shr(x, pow2):
    return lax.shift_right_logical(x, jnp.full(x.shape, int(math.log2(pow2)), jnp.int32))


def _adaln_kernel(ct_ref, w_ref, b_ref, o_ref):
    w = w_ref[...]
    tn = w.shape[1]
    rows = []
    for b in range(ct_ref.shape[0]):
        cb = _silu(ct_ref[b])
        parts = [jnp.sum(w[:, j * LANES:(j + 1) * LANES] * cb, axis=0, keepdims=True)
                 for j in range(tn // LANES)]
        rows.append(jnp.concatenate(parts, axis=1))
    o_ref[...] = jnp.concatenate(rows, axis=0) + b_ref[...]


def adaln_all(c, ada_w, ada_b):
    B, D = c.shape
    depth, _, n_out = ada_w.shape
    tn = 1024
    ct = jnp.broadcast_to(c[:, :, None], (B, D, LANES))
    return pl.pallas_call(
        _adaln_kernel,
        grid=(depth, n_out // tn),
        in_specs=[pl.BlockSpec((B, D, LANES), lambda i, j: (0, 0, 0)),
                  pl.BlockSpec((None, D, tn), lambda i, j: (i, 0, j)),
                  pl.BlockSpec((None, 1, tn), lambda i, j: (i, 0, j))],
        out_specs=pl.BlockSpec((None, B, tn), lambda i, j: (i, 0, j)),
        out_shape=jax.ShapeDtypeStruct((depth, B, n_out), F32),
        compiler_params=_cparams(("arbitrary", "arbitrary")),
        name="adaln",
    )(ct, ada_w, ada_b.reshape(depth, 1, n_out))


def _mod_kernel(x_ref, sc_ref, sh_ref, h_ref):
    h_ref[...] = (x_ref[...] * (1.0 + sc_ref[...]) + sh_ref[...]).astype(h_ref.dtype)


def modulate(x2, sc, sh, T, h_dtype):
    N, D = x2.shape
    tm = 512
    vec = pl.BlockSpec((None, 1, D), lambda i: ((i * tm) // T, 0, 0))
    return pl.pallas_call(
        _mod_kernel,
        grid=(N // tm,),
        in_specs=[pl.BlockSpec((tm, D), lambda i: (i, 0)), vec, vec],
        out_specs=pl.BlockSpec((tm, D), lambda i: (i, 0)),
        out_shape=jax.ShapeDtypeStruct((N, D), h_dtype),
        compiler_params=_cparams(("arbitrary",)),
        name="modulate",
    )(x2, sc, sh)


def _deepnorm(x, y, gate, lg, lb):
    z = ALPHA * x + (1.0 + gate) * y
    mu = jnp.mean(z, axis=-1, keepdims=True)
    zc = z - mu
    var = jnp.mean(zc * zc, axis=-1, keepdims=True)
    return zc * lax.rsqrt(var + LN_EPS) * lg + lb


def _norm_mod_kernel(x_ref, y_ref, g_ref, lg_ref, lb_ref, sc_ref, sh_ref, xo_ref, ho_ref):
    xn = _deepnorm(x_ref[...], y_ref[...].astype(F32), g_ref[...], lg_ref[...], lb_ref[...])
    xo_ref[...] = xn
    ho_ref[...] = (xn * (1.0 + sc_ref[...]) + sh_ref[...]).astype(ho_ref.dtype)


def _norm_kernel(x_ref, y_ref, g_ref, lg_ref, lb_ref, xo_ref):
    xo_ref[...] = _deepnorm(x_ref[...], y_ref[...].astype(F32), g_ref[...], lg_ref[...], lb_ref[...])


def norm_mod(x2, y, gate, lg, lb, sc, sh, T, h_dtype):
    N, D = x2.shape
    tm = 256
    row = pl.BlockSpec((tm, D), lambda i: (i, 0))
    vec = pl.BlockSpec((None, 1, D), lambda i: ((i * tm) // T, 0, 0))
    par = pl.BlockSpec((1, D), lambda i: (0, 0))
    if sc is None:
        return pl.pallas_call(
            _norm_kernel, grid=(N // tm,),
            in_specs=[row, row, vec, par, par], out_specs=row,
            out_shape=jax.ShapeDtypeStruct((N, D), F32),
            compiler_params=_cparams(("arbitrary",)), name="deepnorm",
        )(x2, y, gate, lg, lb), None
    return pl.pallas_call(
        _norm_mod_kernel, grid=(N // tm,),
        in_specs=[row, row, vec, par, par, vec, vec], out_specs=[row, row],
        out_shape=[jax.ShapeDtypeStruct((N, D), F32), jax.ShapeDtypeStruct((N, D), h_dtype)],
        compiler_params=_cparams(("arbitrary",)), name="deepnorm_mod",
    )(x2, y, gate, lg, lb, sc, sh)


def _mm_kernel(x_ref, w_ref, o_ref, *, scale):
    acc = _dot(x_ref[...], w_ref[...])
    if scale != 1.0:
        acc = acc * scale
    o_ref[...] = acc.astype(o_ref.dtype)


def matmul(x, w, out_dtype, *, tm=1024, tn=512, scale=1.0, name="matmul"):
    M, K = x.shape
    N = w.shape[1]
    tn = min(tn, N)
    return pl.pallas_call(
        functools.partial(_mm_kernel, scale=scale),
        grid=(M // tm, N // tn),
        in_specs=[pl.BlockSpec((tm, K), lambda i, j: (i, 0)),
                  pl.BlockSpec((K, tn), lambda i, j: (0, j))],
        out_specs=pl.BlockSpec((tm, tn), lambda i, j: (i, j)),
        out_shape=jax.ShapeDtypeStruct((M, N), out_dtype),
        compiler_params=_cparams(("arbitrary", "arbitrary")),
        name=name,
    )(x, w)


def _mm_glu_kernel(x_ref, wa_ref, wb_ref, o_ref):
    x = x_ref[...]
    a = _dot(x, wa_ref[...])
    b = _dot(x, wb_ref[...])
    o_ref[...] = (a * _sigmoid(b)).astype(o_ref.dtype)


def matmul_glu(x, w, out_dtype, *, tm=1024, tn=512):
    M, K = x.shape
    n = w.shape[1] // 2
    nj = n // tn
    return pl.pallas_call(
        _mm_glu_kernel,
        grid=(M // tm, nj),
        in_specs=[pl.BlockSpec((tm, K), lambda i, j: (i, 0)),
                  pl.BlockSpec((K, tn), lambda i, j: (0, j)),
                  pl.BlockSpec((K, tn), lambda i, j: (0, j + nj))],
        out_specs=pl.BlockSpec((tm, tn), lambda i, j: (i, j)),
        out_shape=jax.ShapeDtypeStruct((M, n), out_dtype),
        compiler_params=_cparams(("arbitrary", "arbitrary")),
        name="matmul_glu",
    )(x, w, w)


def _ffn_kernel(x_ref, w1_ref, w3_ref, w2_ref, o_ref):
    f = pl.program_id(1)
    x = x_ref[...]
    a = _dot(x, w1_ref[...].astype(BF16))
    b = _dot(x, w3_ref[...].astype(BF16))
    part = _dot((_silu(a) * b).astype(BF16), w2_ref[...].astype(BF16))

    @pl.when(f == 0)
    def _():
        o_ref[...] = part

    @pl.when(f > 0)
    def _():
        o_ref[...] += part


def ffn_swiglu(h, w1, w3, w2, *, tm=1024, tf=256):
    N, D = h.shape
    Fd = w1.shape[1]
    single = pl.Buffered(1)
    return pl.pallas_call(
        _ffn_kernel,
        grid=(N // tm, Fd // tf),
        in_specs=[pl.BlockSpec((tm, D), lambda i, f: (i, 0), pipeline_mode=single),
                  pl.BlockSpec((D, tf), lambda i, f: (0, f)),
                  pl.BlockSpec((D, tf), lambda i, f: (0, f)),
                  pl.BlockSpec((tf, D), lambda i, f: (f, 0))],
        out_specs=pl.BlockSpec((tm, D), lambda i, f: (i, 0), pipeline_mode=single),
        out_shape=jax.ShapeDtypeStruct((N, D), F32),
        compiler_params=_cparams(("arbitrary", "arbitrary")),
        name="ffn_swiglu",
    )(h, w1, w3, w2)


def _router_kernel(h_ref, rh_ref, rl_ref, idx_ref, wt_ref):
    h = h_ref[...]
    hh = h.astype(BF16)
    hl = (h - hh.astype(F32)).astype(BF16)
    rh = rh_ref[...]
    logits = _dot(hh, rh) + _dot(hl, rh) + _dot(hh, rl_ref[...])
    lane = lax.broadcasted_iota(jnp.int32, logits.shape, 1)
    lanef = lane.astype(F32)
    logits = jnp.where(lane < N_EXPERTS, logits, -jnp.inf)
    m1 = jnp.max(logits, axis=1, keepdims=True)
    i1 = jnp.min(jnp.where(logits == m1, lanef, float(LANES)), axis=1, keepdims=True)
    rest = jnp.where(lanef == i1, -jnp.inf, logits)
    m2 = jnp.max(rest, axis=1, keepdims=True)
    i2 = jnp.min(jnp.where(rest == m2, lanef, float(LANES)), axis=1, keepdims=True)
    e2 = jnp.exp(m2 - m1)
    w1 = 1.0 / (1.0 + e2)
    w2 = e2 / (1.0 + e2)
    idx_ref[...] = jnp.where(lane == 0, i1, jnp.where(lane == 1, i2, 0.0)).astype(jnp.int32)
    wt_ref[...] = jnp.where(lane == 0, w1, jnp.where(lane == 1, w2, 0.0))


def moe_router(h, router):
    N, D = h.shape
    tm = 512
    rp = jnp.zeros((D, LANES), F32).at[:, :N_EXPERTS].set(router)
    rh = rp.astype(BF16)
    rl = (rp - rh.astype(F32)).astype(BF16)
    row = pl.BlockSpec((tm, LANES), lambda i: (i, 0))
    return pl.pallas_call(
        _router_kernel, grid=(N // tm,),
        in_specs=[pl.BlockSpec((tm, D), lambda i: (i, 0)),
                  pl.BlockSpec((D, LANES), lambda i: (0, 0)),
                  pl.BlockSpec((D, LANES), lambda i: (0, 0))],
        out_specs=[row, row],
        out_shape=[jax.ShapeDtypeStruct((N, LANES), jnp.int32), jax.ShapeDtypeStruct((N, LANES), F32)],
        compiler_params=_cparams(("arbitrary",)), name="moe_router",
    )(h, rh, rl)


def _row_copy(src_hbm, dst_ref, sem, src_row, dst_row):
    return pltpu.make_async_copy(src_hbm.at[pl.ds(src_row, 1), :], dst_ref.at[pl.ds(dst_row, 1), :], sem)


def _row_put(src_ref, dst_hbm, sem, src_row, dst_row):
    return pltpu.make_async_copy(src_ref.at[pl.ds(src_row, 1), :], dst_hbm.at[pl.ds(dst_row, 1), :], sem)


def _dispatch_rows_kernel(pos_ref, x_ref, init_hbm, o_hbm, sem, *, rows):
    del init_hbm
    base = pl.program_id(0) * rows

    def start(r, carry):
        for k in range(TOP_K):
            _row_put(x_ref, o_hbm, sem, r, pos_ref[TOP_K * (base + r) + k]).start()
        return carry

    def wait(r, carry):
        for k in range(TOP_K):
            _row_put(x_ref, o_hbm, sem, r, pos_ref[TOP_K * (base + r) + k]).wait()
        return carry

    lax.fori_loop(0, rows, start, 0)
    lax.fori_loop(0, rows, wait, 0)


def dispatch_rows(x, pos, n_out, *, rows=256):
    N, D = x.shape
    return pl.pallas_call(
        functools.partial(_dispatch_rows_kernel, rows=rows),
        grid_spec=pltpu.PrefetchScalarGridSpec(
            num_scalar_prefetch=1, grid=(N // rows,),
            in_specs=[pl.BlockSpec((rows, D), lambda i, p: (i, 0)), pl.BlockSpec(memory_space=pl.ANY)],
            out_specs=pl.BlockSpec(memory_space=pl.ANY),
            scratch_shapes=[pltpu.SemaphoreType.DMA(())]),
        out_shape=jax.ShapeDtypeStruct((n_out, D), x.dtype),
        input_output_aliases={2: 0},
        compiler_params=_cparams(("arbitrary",)), name="moe_dispatch",
    )(pos, x, jnp.zeros((n_out, D), x.dtype))


def _moe_mm_kernel(te_ref, tv_ref, x_ref, w1_ref, w3_ref, w2_ref, o_ref, xb_ref):
    i = pl.program_id(0)
    f = pl.program_id(1)

    @pl.when(f == 0)
    def _():
        xb_ref[...] = x_ref[...].astype(BF16)
        o_ref[...] = jnp.zeros_like(o_ref)

    @pl.when(tv_ref[i] > 0)
    def _():
        x = xb_ref[...]
        a = _dot(x, w1_ref[...].astype(BF16))
        b = _dot(x, w3_ref[...].astype(BF16))
        o_ref[...] += _dot((_silu(a) * b).astype(BF16), w2_ref[...].astype(BF16))


def moe_grouped_swiglu(xs, tile_expert, tile_valid, w1, w3, w2, *, tm, tf=256):
    R, D = xs.shape
    Fd = w1.shape[2]
    single = pl.Buffered(1)
    return pl.pallas_call(
        _moe_mm_kernel,
        grid_spec=pltpu.PrefetchScalarGridSpec(
            num_scalar_prefetch=2, grid=(R // tm, Fd // tf),
            in_specs=[pl.BlockSpec((tm, D), lambda i, f, te, tv: (i, 0), pipeline_mode=single),
                      pl.BlockSpec((None, D, tf), lambda i, f, te, tv: (te[i], 0, f * tv[i])),
                      pl.BlockSpec((None, D, tf), lambda i, f, te, tv: (te[i], 0, f * tv[i])),
                      pl.BlockSpec((None, tf, D), lambda i, f, te, tv: (te[i], f * tv[i], 0))],
            out_specs=pl.BlockSpec((tm, D), lambda i, f, te, tv: (i, 0), pipeline_mode=single),
            scratch_shapes=[pltpu.VMEM((tm, D), BF16)]),
        out_shape=jax.ShapeDtypeStruct((R, D), F32),
        compiler_params=_cparams(("arbitrary", "arbitrary"), vmem_mib=52), name="moe_grouped_swiglu",
    )(tile_expert, tile_valid, xs, w1, w3, w2)


def _combine_gather(pos_ref, ys_hbm, buf_ref, sem, base, rows):
    def start(r, carry):
        _row_copy(ys_hbm, buf_ref.at[0], sem, pos_ref[2 * (base + r)], r).start()
        _row_copy(ys_hbm, buf_ref.at[1], sem, pos_ref[2 * (base + r) + 1], r).start()
        return carry

    def wait(r, carry):
        _row_copy(ys_hbm, buf_ref.at[0], sem, pos_ref[2 * (base + r)], r).wait()
        _row_copy(ys_hbm, buf_ref.at[1], sem, pos_ref[2 * (base + r) + 1], r).wait()
        return carry

    lax.fori_loop(0, rows, start, 0)
    lax.fori_loop(0, rows, wait, 0)


def _moe_combine(wt_ref, buf_ref):
    wt = wt_ref[...]
    return wt[:, 0:1] * buf_ref[0] + wt[:, 1:2] * buf_ref[1]


def _combine_norm_mod_kernel(pos_ref, ys_hbm, wt_ref, x_ref, g_ref, lg_ref, lb_ref, sc_ref, sh_ref,
                             xo_ref, ho_ref, buf_ref, sem, *, rows):
    _combine_gather(pos_ref, ys_hbm, buf_ref, sem, pl.program_id(0) * rows, rows)
    xn = _deepnorm(x_ref[...], _moe_combine(wt_ref, buf_ref), g_ref[...], lg_ref[...], lb_ref[...])
    xo_ref[...] = xn
    ho_ref[...] = (xn * (1.0 + sc_ref[...]) + sh_ref[...]).astype(ho_ref.dtype)


def _combine_norm_kernel(pos_ref, ys_hbm, wt_ref, x_ref, g_ref, lg_ref, lb_ref, xo_ref, buf_ref, sem, *, rows):
    _combine_gather(pos_ref, ys_hbm, buf_ref, sem, pl.program_id(0) * rows, rows)
    xo_ref[...] = _deepnorm(x_ref[...], _moe_combine(wt_ref, buf_ref), g_ref[...], lg_ref[...], lb_ref[...])


def moe_combine_norm_mod(ys, pos, wts, x2, gate, lg, lb, sc, sh, T, h_dtype, *, rows=256):
    N, D = x2.shape
    row = pl.BlockSpec((rows, D), lambda i, p: (i, 0))
    vec = pl.BlockSpec((None, 1, D), lambda i, p: ((i * rows) // T, 0, 0))
    par = pl.BlockSpec((1, D), lambda i, p: (0, 0))
    wsp = pl.BlockSpec((rows, LANES), lambda i, p: (i, 0))
    anysp = pl.BlockSpec(memory_space=pl.ANY)
    scratch = [pltpu.VMEM((2, rows, D), F32), pltpu.SemaphoreType.DMA(())]
    if sc is None:
        out = pl.pallas_call(
            functools.partial(_combine_norm_kernel, rows=rows),
            grid_spec=pltpu.PrefetchScalarGridSpec(
                num_scalar_prefetch=1, grid=(N // rows,),
                in_specs=[anysp, wsp, row, vec, par, par], out_specs=row, scratch_shapes=scratch),
            out_shape=jax.ShapeDtypeStruct((N, D), F32),
            compiler_params=_cparams(("arbitrary",)), name="moe_combine_norm",
        )(pos, ys, wts, x2, gate, lg, lb)
        return out, None
    return pl.pallas_call(
        functools.partial(_combine_norm_mod_kernel, rows=rows),
        grid_spec=pltpu.PrefetchScalarGridSpec(
            num_scalar_prefetch=1, grid=(N // rows,),
            in_specs=[anysp, wsp, row, vec, par, par, vec, vec], out_specs=[row, row],
            scratch_shapes=scratch),
        out_shape=[jax.ShapeDtypeStruct((N, D), F32), jax.ShapeDtypeStruct((N, D), h_dtype)],
        compiler_params=_cparams(("arbitrary",)), name="moe_combine_norm_mod",
    )(pos, ys, wts, x2, gate, lg, lb, sc, sh)


def _route_tables(idx2, tm, n_tiles):
    e_flat = idx2.reshape(-1)
    onehot = (jnp.arange(N_EXPERTS, dtype=jnp.int32)[:, None] == e_flat[None, :]).astype(jnp.int32)
    csum = jnp.cumsum(onehot, axis=1)
    counts = csum[:, -1]
    rank = jnp.sum(onehot * (csum - 1), axis=0)
    tiles_per = (counts + tm - 1) // tm
    tile_end = jnp.cumsum(tiles_per)
    pad_start = (tile_end - tiles_per) * tm
    t = jnp.arange(n_tiles)
    te = jnp.sum((t[:, None] >= tile_end[None, :]).astype(jnp.int32), axis=1)
    tile_valid = (te < N_EXPERTS).astype(jnp.int32)
    tile_expert = jnp.minimum(te, N_EXPERTS - 1).astype(jnp.int32)
    pos = (jnp.sum(onehot * pad_start[:, None], axis=0) + rank).astype(jnp.int32)
    return tile_expert, tile_valid, pos


def moe_sublayer(h, x2, router, w1, w3, w2, gate, lg, lb, sc, sh, T, h_dtype, *, tm=1024):
    N, D = h.shape
    idx, wts = moe_router(h, router)
    n_tiles = (N * TOP_K) // tm + N_EXPERTS
    tile_expert, tile_valid, pos = _route_tables(idx[:, :TOP_K], tm, n_tiles)
    xs = dispatch_rows(h, pos, n_tiles * tm)
    ys = moe_grouped_swiglu(xs, tile_expert, tile_valid, w1, w3, w2, tm=tm)
    return moe_combine_norm_mod(ys, pos, wts, x2, gate, lg, lb, sc, sh, T, h_dtype)


DW_HALO = 32
DW_ROWS = 128
DW_PITCH = 3
DW_GROUP = 4


def _dwconv_kernel(u_ref, halo_ref, dw_ref, lg_ref, lb_ref, o_ref, ext_ref, acc_ref, *, tiles_per_seq):
    i = pl.program_id(0)
    first = (i % tiles_per_seq) == 0
    lead = DW_HALO - (CONV_WIDTH - 1)
    P = DW_PITCH
    lg = lg_ref[...]
    lb = lb_ref[...]
    for sl in range(u_ref.shape[1] // LANES):
        cols = slice(sl * LANES, (sl + 1) * LANES)
        ext_ref[sl, pl.ds(0, DW_HALO, stride=P), :] = jnp.where(first, 0.0, halo_ref[:, cols].astype(F32))
        ext_ref[sl, pl.ds(P * DW_HALO, DW_ROWS, stride=P), :] = u_ref[:, cols].astype(F32)
    for sl in range(u_ref.shape[1] // LANES):
        cols = slice(sl * LANES, (sl + 1) * LANES)
        for rg in range(DW_ROWS // (SUBLANES * DW_GROUP)):
            accs = [None] * DW_GROUP
            for k in range(CONV_WIDTH):
                w = dw_ref[k:k + 1, cols]
                for t in range(DW_GROUP):
                    r0 = (rg * DW_GROUP + t) * SUBLANES
                    term = ext_ref[sl, pl.ds(P * (r0 + lead + k), SUBLANES, stride=P), :] * w
                    accs[t] = term if accs[t] is None else accs[t] + term
            for t in range(DW_GROUP):
                r0 = (rg * DW_GROUP + t) * SUBLANES
                acc_ref[r0:r0 + SUBLANES, cols] = accs[t]
    for rc in range(DW_ROWS // SUBLANES):
        r0 = rc * SUBLANES
        acc = acc_ref[r0:r0 + SUBLANES, :]
        mu = jnp.mean(acc, axis=-1, keepdims=True)
        zc = acc - mu
        var = jnp.mean(zc * zc, axis=-1, keepdims=True)
        v = zc * lax.rsqrt(var + LN_EPS) * lg + lb
        o_ref[r0:r0 + SUBLANES, :] = _silu(v).astype(o_ref.dtype)


def dwconv_ln_silu(u, dw, lg, lb, T):
    N, D = u.shape
    hb = DW_ROWS // DW_HALO
    return pl.pallas_call(
        functools.partial(_dwconv_kernel, tiles_per_seq=T // DW_ROWS),
        grid=(N // DW_ROWS,),
        in_specs=[pl.BlockSpec((DW_ROWS, D), lambda i: (i, 0)),
                  pl.BlockSpec((DW_HALO, D), lambda i: (jnp.maximum(i * hb - 1, 0), 0)),
                  pl.BlockSpec((CONV_WIDTH, D), lambda i: (0, 0)),
                  pl.BlockSpec((1, D), lambda i: (0, 0)),
                  pl.BlockSpec((1, D), lambda i: (0, 0))],
        out_specs=pl.BlockSpec((DW_ROWS, D), lambda i: (i, 0)),
        out_shape=jax.ShapeDtypeStruct((N, D), BF16),
        scratch_shapes=[pltpu.VMEM((D // LANES, DW_PITCH * (DW_HALO + DW_ROWS), LANES), F32),
                        pltpu.VMEM((DW_ROWS, D), F32)],
        compiler_params=_cparams(("arbitrary",)), name="dwconv_ln_silu",
    )(u, u, dw, lg.reshape(1, D), lb.reshape(1, D))


def conformer_mixer(h, w_in, dw, lg, lb, w_out, T):
    u = matmul_glu(h, w_in.astype(BF16), BF16)
    v = dwconv_ln_silu(u, dw, lg, lb, T)
    return matmul(v, w_out.astype(BF16), F32, name="conv_out")


GC_HALO = 16
GC_ROWS = 512
GC_COLS = 1024


def _gconv_kernel(x_ref, halo_ref, w_ref, o_ref, ext_ref, *, tiles_per_seq):
    i = pl.program_id(0)
    first = (i % tiles_per_seq) == 0
    ext_ref[0:GC_HALO, :] = jnp.where(first, 0.0, halo_ref[...].astype(F32))
    ext_ref[GC_HALO:, :] = x_ref[...].astype(F32)
    lead = GC_HALO - (GDN_CONV - 1)
    acc = None
    for k in range(GDN_CONV):
        term = ext_ref[lead + k:lead + k + GC_ROWS, :] * w_ref[k:k + 1, :]
        acc = term if acc is None else acc + term
    o_ref[...] = _silu(acc).astype(o_ref.dtype)


def gdn_conv_silu(x, w, T):
    N, C = x.shape
    hb = GC_ROWS // GC_HALO
    return pl.pallas_call(
        functools.partial(_gconv_kernel, tiles_per_seq=T // GC_ROWS),
        grid=(N // GC_ROWS, C // GC_COLS),
        in_specs=[pl.BlockSpec((GC_ROWS, GC_COLS), lambda i, j: (i, j)),
                  pl.BlockSpec((GC_HALO, GC_COLS), lambda i, j: (jnp.maximum(i * hb - 1, 0), j)),
                  pl.BlockSpec((GDN_CONV, GC_COLS), lambda i, j: (0, j))],
        out_specs=pl.BlockSpec((GC_ROWS, GC_COLS), lambda i, j: (i, j)),
        out_shape=jax.ShapeDtypeStruct((N, C), BF16),
        scratch_shapes=[pltpu.VMEM((GC_HALO + GC_ROWS, GC_COLS), F32)],
        compiler_params=_cparams(("arbitrary", "arbitrary")), name="gdn_conv_silu",
    )(x, x, w)


GP_ROWS = 512
GP_SUB = 256


def _l2norm(t):
    return t * lax.rsqrt(jnp.sum(t * t, axis=-1, keepdims=True) + 1e-6)


def _gdn_prep_kernel(q_ref, k_ref, v_ref, braw_ref, araw_ref, alog_ref, dtb_ref,
                     u_ref, w_ref, qg_ref, kdt_ref, attn_ref, gl_ref):
    C = GDN_CHUNK
    S = GP_SUB
    ri = lax.broadcasted_iota(jnp.int32, (S, S), 0)
    ci = lax.broadcasted_iota(jnp.int32, (S, S), 1)
    same = _shr(ri, C) == _shr(ci, C)
    lower = same & (ri >= ci)
    strict = same & (ri > ci)
    eye = ri == ci
    cum_mat = (same & (ri <= ci)).astype(BF16)
    tot_mat = same.astype(BF16)
    eye_f = eye.astype(F32)

    def to_col(row):
        return jnp.sum(jnp.where(eye, jnp.broadcast_to(row, (S, S)), 0.0), axis=1, keepdims=True)

    n_sb = GP_ROWS // S
    chains = [(sb, e) for sb in range(n_sb) for e in range(2)]
    qs, ks, grams, qks = [], [], [], []
    for sb in range(n_sb):
        rows = slice(sb * S, (sb + 1) * S)
        q = _l2norm(q_ref[rows, :].astype(F32)) * (HEAD_DIM ** -0.5)
        k = _l2norm(k_ref[rows, :].astype(F32))
        kb16 = k.astype(BF16)
        qs.append(q)
        ks.append(k)
        grams.append(_dot_nt(kb16, kb16))
        qks.append(_dot_nt(q.astype(BF16), kb16))

    gcums, gtots, betas, decays, invs, pws = [], [], [], [], [], []
    for sb, e in chains:
        rows = slice(sb * S, (sb + 1) * S)
        a_row = araw_ref[e, :, rows]
        b_row = braw_ref[e, :, rows]
        g_row = -jnp.exp(alog_ref[e, :, 0:1]) * _softplus(a_row + dtb_ref[e, :, 0:1])
        g8 = jnp.broadcast_to(g_row, (SUBLANES, S))
        gcum_row = _dot_x3(g8, cum_mat)[0:1, :]
        gcum = to_col(gcum_row)
        gtots.append(to_col(_dot_x3(g8, tot_mat)[0:1, :]))
        beta = to_col(_sigmoid(b_row))
        decay = jnp.exp(jnp.where(lower, gcum - gcum_row, NEG_INF))
        a = jnp.where(strict, beta * grams[sb] * decay, 0.0)
        gcums.append(gcum)
        betas.append(beta)
        decays.append(decay)
        invs.append(eye_f - a)
        pws.append(a)

    for _ in range(int(math.log2(C)) - 1):
        pw16 = [pw.astype(BF16) for pw in pws]
        pws = [_dot(p16, p16) for p16 in pw16]
        invs = [inv + _dot(inv.astype(BF16), pw.astype(BF16)) for inv, pw in zip(invs, pws)]

    for ci_, (sb, e) in enumerate(chains):
        rows = slice(sb * S, (sb + 1) * S)
        cols = slice(e * HEAD_DIM, (e + 1) * HEAD_DIM)
        q, k = qs[sb], ks[sb]
        gcum, gtot, beta = gcums[ci_], gtots[ci_], betas[ci_]
        eg = jnp.exp(gcum)
        v = v_ref[rows, cols].astype(F32)
        inv16 = invs[ci_].astype(BF16)
        u_ref[rows, cols] = _dot(inv16, (v * beta).astype(BF16))
        w_ref[rows, cols] = _dot(inv16, (k * (beta * eg)).astype(BF16)).astype(BF16)
        qg_ref[rows, cols] = (q * eg).astype(BF16)
        kdec = k * jnp.exp(gtot - gcum)
        kdt_ref[e, :, rows] = kdec.T.astype(BF16)
        attn = qks[sb] * decays[ci_]
        for c in range(S // C):
            attn_ref[e, sb * S + c * C:sb * S + (c + 1) * C, :] = (
                attn[c * C:(c + 1) * C, c * C:(c + 1) * C].astype(BF16))
            gl_ref[e, sb * (S // C) + c:sb * (S // C) + c + 1, :] = jnp.broadcast_to(
                jnp.exp(gtot[c * C:c * C + 1, :]), (1, LANES))


def gdn_prep(qkv, baT, a_log, dt_bias, B, T):
    N = qkv.shape[0]
    HK, HV, dh = GDN_QK_HEADS, GDN_V_HEADS, HEAD_DIM
    nt = T // GP_ROWS
    nc = GP_ROWS // GDN_CHUNK
    rowmap = lambda b, hk, t: (b * nt + t)
    alog = jnp.broadcast_to(a_log.astype(F32)[:, None, None], (HV, 1, LANES))
    dtb = jnp.broadcast_to(dt_bias.astype(F32)[:, None, None], (HV, 1, LANES))
    out_shape = [jax.ShapeDtypeStruct((N, HV * dh), F32),
                 jax.ShapeDtypeStruct((N, HV * dh), BF16),
                 jax.ShapeDtypeStruct((N, HV * dh), BF16),
                 jax.ShapeDtypeStruct((B, HV, dh, T), BF16),
                 jax.ShapeDtypeStruct((B, HV, T, GDN_CHUNK), BF16),
                 jax.ShapeDtypeStruct((B, HV, T // GDN_CHUNK, LANES), F32)]
    big = pl.BlockSpec((GP_ROWS, 2 * dh), lambda b, hk, t: (rowmap(b, hk, t), hk))
    return pl.pallas_call(
        _gdn_prep_kernel,
        grid=(B, HK, nt),
        in_specs=[pl.BlockSpec((GP_ROWS, dh), lambda b, hk, t: (rowmap(b, hk, t), hk)),
                  pl.BlockSpec((GP_ROWS, dh), lambda b, hk, t: (rowmap(b, hk, t), HK + hk)),
                  pl.BlockSpec((GP_ROWS, 2 * dh), lambda b, hk, t: (rowmap(b, hk, t), HK + hk)),
                  pl.BlockSpec((None, 2, 1, GP_ROWS), lambda b, hk, t: (b, hk, 0, t)),
                  pl.BlockSpec((None, 2, 1, GP_ROWS), lambda b, hk, t: (b, HK + hk, 0, t)),
                  pl.BlockSpec((2, 1, LANES), lambda b, hk, t: (hk, 0, 0)),
                  pl.BlockSpec((2, 1, LANES), lambda b, hk, t: (hk, 0, 0))],
        out_specs=[big, big, big,
                   pl.BlockSpec((None, 2, dh, GP_ROWS), lambda b, hk, t: (b, hk, 0, t)),
                   pl.BlockSpec((None, 2, GP_ROWS, GDN_CHUNK), lambda b, hk, t: (b, hk, t, 0)),
                   pl.BlockSpec((None, 2, nc, LANES), lambda b, hk, t: (b, hk, t, 0))],
        out_shape=out_shape,
        compiler_params=_cparams(("arbitrary", "arbitrary", "arbitrary")), name="gdn_prep",
    )(qkv, qkv, qkv, baT, baT, alog, dtb)


GS_HEADS = 4
GS_ROWS = 512


def _gdn_scan_kernel(u_ref, w_ref, qg_ref, kdt_ref, attn_ref, gl_ref, z_ref, nw_ref, o_ref, s_ref):
    C = GDN_CHUNK

    @pl.when(pl.program_id(2) == 0)
    def _():
        s_ref[...] = jnp.zeros_like(s_ref)

    nw = nw_ref[...]
    for c in range(GS_ROWS // C):
        rows = slice(c * C, (c + 1) * C)
        for hh in range(GS_HEADS):
            cols = slice(hh * HEAD_DIM, (hh + 1) * HEAD_DIM)
            s = s_ref[hh]
            lhs = jnp.concatenate([w_ref[rows, cols], qg_ref[rows, cols]], axis=0)
            r = _dot(lhs, s.astype(BF16))
            v_new = u_ref[rows, cols] - r[:C]
            lhs2 = jnp.concatenate([attn_ref[hh, rows, :], kdt_ref[hh, :, rows]], axis=0)
            r2 = _dot(lhs2, v_new.astype(BF16))
            o = r[C:] + r2[:C]
            s_ref[hh] = s * gl_ref[hh, c:c + 1, :] + r2[C:]
            z = z_ref[rows, cols].astype(F32)
            o = o * lax.rsqrt(jnp.mean(o * o, axis=-1, keepdims=True) + 1e-6) * nw * _silu(z)
            o_ref[rows, cols] = o.astype(o_ref.dtype)


def gdn_scan(u, w, qg, kdt, attn, gl, z, norm_w, B, T):
    N = u.shape[0]
    HV, dh = GDN_V_HEADS, HEAD_DIM
    nt = T // GS_ROWS
    nc = GS_ROWS // GDN_CHUNK
    wide = pl.BlockSpec((GS_ROWS, GS_HEADS * dh), lambda b, g, t: (b * nt + t, g))
    return pl.pallas_call(
        _gdn_scan_kernel,
        grid=(B, HV // GS_HEADS, nt),
        in_specs=[wide, wide, wide,
                  pl.BlockSpec((None, GS_HEADS, dh, GS_ROWS), lambda b, g, t: (b, g, 0, t)),
                  pl.BlockSpec((None, GS_HEADS, GS_ROWS, GDN_CHUNK), lambda b, g, t: (b, g, t, 0)),
                  pl.BlockSpec((None, GS_HEADS, nc, LANES), lambda b, g, t: (b, g, t, 0)),
                  wide,
                  pl.BlockSpec((1, dh), lambda b, g, t: (0, 0))],
        out_specs=wide,
        out_shape=jax.ShapeDtypeStruct((N, HV * dh), BF16),
        scratch_shapes=[pltpu.VMEM((GS_HEADS, dh, dh), F32)],
        compiler_params=_cparams(("arbitrary", "arbitrary", "arbitrary")), name="gdn_scan",
    )(u, w, qg, kdt, attn, gl, z, norm_w.astype(F32).reshape(1, dh))


def gdn_mixer(h, w_in, conv_w, a_log, dt_bias, norm_w, w_out, B, T):
    N, D = h.shape
    qkv_dim = 2 * GDN_QK_HEADS * HEAD_DIM + GDN_V_HEADS * HEAD_DIM
    v_dim = GDN_V_HEADS * HEAD_DIM
    w_qkv = w_in[:, :qkv_dim].astype(BF16)
    w_z = w_in[:, qkv_dim:qkv_dim + v_dim].astype(BF16)
    w_ba = jnp.zeros((D, LANES), BF16).at[:, :2 * GDN_V_HEADS].set(w_in[:, qkv_dim + v_dim:].astype(BF16))
    qkv = matmul(h, w_qkv, BF16, name="gdn_in_qkv")
    z = matmul(h, w_z, BF16, name="gdn_in_z")
    ba = matmul(h, w_ba, F32, name="gdn_in_ba")
    qkv = gdn_conv_silu(qkv, conv_w, T)
    baT = ba[:, :2 * GDN_V_HEADS].reshape(B, T, 2 * GDN_V_HEADS).transpose(0, 2, 1)[:, :, None, :]
    u, w, qg, kdt, attn, gl = gdn_prep(qkv, baT, a_log, dt_bias, B, T)
    o = gdn_scan(u, w, qg, kdt, attn, gl, z, norm_w, B, T)
    return matmul(o, w_out.astype(BF16), F32, name="gdn_out")


def _rel_bucket(dist):
    n = jnp.maximum(dist, 0)
    large = REL_EXACT + (jnp.log(jnp.maximum(n, 1).astype(F32) / REL_EXACT)
                         / math.log(REL_MAX_DIST / REL_EXACT) * (REL_BUCKETS - REL_EXACT)).astype(jnp.int32)
    return jnp.where(n < REL_EXACT, n, jnp.minimum(large, REL_BUCKETS - 1))


N_TOEP = 9
BIAS_PAD = 144


def _toeplitz(rows_rev):
    lead = rows_rev.shape[:-1]
    n = Q_BLOCK
    t = jnp.tile(rows_rev, (1,) * len(lead) + (n,))[..., :n * 2 * n].reshape(lead + (n, 2 * n))
    return t[..., n:]


def _nsa_bias_tables(rel_bias, n_cmp):
    n_max = Q_BLOCK * N_TOEP + BIAS_PAD
    dist = jnp.arange(-BIAS_PAD, n_max + 1)
    far = rel_bias.astype(F32)[REL_BUCKETS - 1]
    ftab = (rel_bias.astype(F32)[_rel_bucket(dist)] - far[None, :]).T

    def family(shift, n_d):
        rows = []
        for d in range(n_d):
            c = Q_BLOCK * d + Q_BLOCK + shift + BIAS_PAD
            rows.append(ftab[:, c - 2 * Q_BLOCK:c + 1][:, ::-1])
        return _toeplitz(jnp.stack(rows, axis=1))

    toep = family(0, N_TOEP)
    toep = toep.reshape(NSA_GROUPS, NSA_HPG, N_TOEP, Q_BLOCK, Q_BLOCK).transpose(0, 2, 1, 3, 4)
    toep = toep.reshape(NSA_GROUPS, N_TOEP, NSA_HPG * Q_BLOCK, Q_BLOCK)
    cm = family(-CMP_STRIDE, N_TOEP - 1)[..., ::CMP_STRIDE]
    strip = jnp.concatenate([cm[:, d] for d in range(N_TOEP - 2, -1, -1)], axis=-1)
    fill = jnp.zeros((NSA_HEADS, Q_BLOCK, n_cmp - strip.shape[-1]), F32)
    cmpb = jnp.concatenate([strip, fill], axis=-1).reshape(NSA_GROUPS, NSA_HPG * Q_BLOCK, n_cmp)
    return toep, cmpb


def _compress_kernel(x_ref, pos_ref, w1a_ref, w1b_ref, w2_ref, o_ref):
    x = x_ref[...].astype(F32)
    pos = pos_ref[...]
    a = _dot((x + pos[0:1, :]).astype(BF16), w1a_ref[...])
    b = _dot((x + pos[1:2, :]).astype(BF16), w1b_ref[...])
    n = x.shape[0]
    hid = a + pltpu.roll(b, n - 1, axis=0)
    o_ref[...] = _dot(_silu(hid).astype(BF16), w2_ref[...]).astype(o_ref.dtype)


def nsa_compress(kv_cmp, cmp_pos, w1k, w2k, w1v, w2v, B, T):
    G, dh = NSA_GROUPS, HEAD_DIM
    half = CMP_BLOCK // 2
    nch = T // half
    x = kv_cmp.reshape(B, nch, half, 2, G, dh).transpose(3, 0, 4, 1, 2, 5).reshape(2, B, G, nch, half * dh)
    pos = cmp_pos.astype(F32).reshape(2, 2, half * dh)
    w1 = jnp.stack([w1k, w1v]).astype(BF16)
    w2 = jnp.stack([w2k, w2v]).astype(BF16)
    return pl.pallas_call(
        _compress_kernel,
        grid=(2, B, G),
        in_specs=[pl.BlockSpec((None, None, None, nch, half * dh), lambda s, b, g: (s, b, g, 0, 0)),
                  pl.BlockSpec((None, 2, half * dh), lambda s, b, g: (s, 0, 0)),
                  pl.BlockSpec((None, half * dh, CMP_HIDDEN), lambda s, b, g: (s, 0, 0)),
                  pl.BlockSpec((None, half * dh, CMP_HIDDEN), lambda s, b, g: (s, 1, 0)),
                  pl.BlockSpec((None, CMP_HIDDEN, dh), lambda s, b, g: (s, 0, 0))],
        out_specs=pl.BlockSpec((None, None, None, nch, dh), lambda s, b, g: (s, b, g, 0, 0)),
        out_shape=jax.ShapeDtypeStruct((2, B, G, nch, dh), BF16),
        compiler_params=_cparams(("arbitrary", "arbitrary", "arbitrary")), name="nsa_compress",
    )(x, pos, w1, w1, w2)


SLC_TILE = 512
SM_ROWS = 64
WIN_KEYS = WINDOW + Q_BLOCK


def _nsa_attn_kernel(q_ref, kc_ref, vc_ref, ksa_ref, vs_ref, kw_ref, vw_ref, gate_ref, toep_ref, cmpb_ref,
                     ovl_ref, o_ref, qa_ref, s0_ref, s1_ref, p_ref, al_ref, oc_ref, ow_ref, acc_ref, m_ref, l_ref):
    qi = pl.program_id(2)
    qs = qi * Q_BLOCK
    HPG, dh, QB = NSA_HPG, HEAD_DIM, Q_BLOCK
    R = HPG * QB
    q = q_ref[...]
    Q = jnp.concatenate([q[:, h * dh:(h + 1) * dh] for h in range(HPG)], axis=0)
    tq = qs + lax.broadcasted_iota(jnp.int32, (QB, 1), 0)

    n_cmp = kc_ref.shape[0]
    s_all = _dot_nt(Q, kc_ref[...])
    shift = lax.rem(8 * (qi - 7) + n_cmp, n_cmp)
    cidx = lax.broadcasted_iota(jnp.int32, (1, n_cmp), 1)
    valid = (cidx * CMP_STRIDE + (CMP_BLOCK - 1)) <= tq
    p4 = None
    pn16 = []
    for h in range(HPG):
        rows = slice(h * QB, (h + 1) * QB)
        bias_h = pltpu.roll(cmpb_ref[rows, :], shift, axis=1)
        s = jnp.where(valid, s_all[rows] + bias_h, NEG_INF)
        m = jnp.max(s, axis=1, keepdims=True)
        p = jnp.where(valid, jnp.exp(s - m), 0.0)
        l = jnp.sum(p, axis=1, keepdims=True)
        pn = p * jnp.where(l > 0.0, 1.0 / jnp.where(l > 0.0, l, 1.0), 0.0)
        p4 = pn if p4 is None else p4 + pn
        pn16.append(pn.astype(BF16))
    oc_ref[...] = _dot(jnp.concatenate(pn16, axis=0), vc_ref[...])

    imp = _dot_x3(p4, ovl_ref[...])
    n_slc = imp.shape[1]
    blk = lax.broadcasted_iota(jnp.int32, (QB, n_slc), 1)
    cur = _shr(tq, SLC_BLOCK)
    forced = (blk == 0) | (blk == cur) | (blk == cur - 1)
    work = jnp.where(forced, FORCE_SCORE, jnp.where(blk <= cur, imp, -1.0))
    work = work.T
    sidx = lax.broadcasted_iota(jnp.int32, (n_slc, QB), 0).astype(F32)
    sel = jnp.zeros((n_slc, QB), F32)
    for _ in range(min(SLC_TOP_N, n_slc)):
        mx = jnp.max(work, axis=0, keepdims=True)
        first = jnp.min(jnp.where(work == mx, sidx, float(n_slc)), axis=0, keepdims=True)
        hit = sidx == first
        sel = jnp.where(hit, 1.0, sel)
        work = jnp.where(hit, -2.0, work)
    selneg = jnp.where(sel > 0.5, 0.0, NEG_INF).T.astype(BF16)
    if n_slc < LANES:
        selneg = jnp.concatenate([selneg, jnp.zeros((QB, LANES - n_slc), BF16)], axis=1)
    qa_ref[:, 0:dh] = Q
    for h in range(HPG):
        qa_ref[h * QB:(h + 1) * QB, dh:] = selneg

    m_ref[...] = jnp.full(m_ref.shape, NEG_INF, F32)
    l_ref[...] = jnp.zeros(l_ref.shape, F32)
    acc_ref[...] = jnp.zeros(acc_ref.shape, F32)
    start = pl.multiple_of(jnp.maximum(qs - WINDOW, 0), QB)
    kwin = kw_ref[pl.ds(start, WIN_KEYS), :]
    vwin = vw_ref[pl.ds(start, WIN_KEYS), :]
    sw_all = _dot_nt(Q, kwin)
    doff = (qs - start) // QB
    n_wt = WIN_KEYS // QB
    wmask = []
    for t in range(n_wt):
        dist = tq - (start + t * QB + lax.broadcasted_iota(jnp.int32, (1, QB), 1))
        wmask.append(jnp.where((dist >= 0) & (dist < WINDOW), 0.0, NEG_INF))
    pw16 = []
    inv_lw = []
    for h in range(HPG):
        rows = slice(h * QB, (h + 1) * QB)
        parts = [sw_all[rows, t * QB:(t + 1) * QB] + toep_ref[jnp.clip(doff - t, 0, N_TOEP - 1), rows, :]
                 + wmask[t] for t in range(n_wt)]
        mw = parts[0]
        for t in range(1, n_wt):
            mw = jnp.maximum(mw, parts[t])
        mw = jnp.max(mw, axis=1, keepdims=True)
        pw = [jnp.exp(pt - mw) for pt in parts]
        lw = pw[0]
        for t in range(1, n_wt):
            lw = lw + pw[t]
        inv_lw.append(1.0 / jnp.sum(lw, axis=1, keepdims=True))
        pw16.append(jnp.concatenate(pw, axis=1).astype(BF16))
    ow_ref[...] = _dot(jnp.concatenate(pw16, axis=0), vwin) * jnp.concatenate(inv_lw, axis=0)

    quarters = SLC_TILE // QB
    n_lane_tiles = SLC_TILE // LANES
    n_tiles = qi // quarters + 1
    last_tile = ksa_ref.shape[0] // SLC_TILE - 1
    j_near = jnp.maximum((qi - (N_TOEP - 2)) // quarters, 0)

    def scores(j, s_ref):
        k0 = pl.multiple_of(jnp.minimum(j, last_tile) * SLC_TILE, SLC_TILE)
        s_ref[...] = _dot_nt(qa_ref[...], ksa_ref[pl.ds(k0, SLC_TILE), :])

    def add_bias(j, s_ref):
        d0 = qi - quarters * j
        kpos = j * SLC_TILE + lax.broadcasted_iota(jnp.int32, (1, SLC_TILE), 1)
        causal = jnp.where(kpos <= tq, 0.0, NEG_INF)
        for h in range(HPG):
            rows = slice(h * QB, (h + 1) * QB)
            delta = jnp.concatenate(
                [toep_ref[jnp.clip(d0 - t, 0, N_TOEP - 1), rows, :] for t in range(quarters)], axis=1)
            s_ref[rows, :] = s_ref[rows, :] + delta + causal

    def softmax_pv(j, s_ref):
        for c in range(R // SM_ROWS):
            rows = slice(c * SM_ROWS, (c + 1) * SM_ROWS)
            parts = [s_ref[rows, t * LANES:(t + 1) * LANES] for t in range(n_lane_tiles)]
            mc = parts[0]
            for t in range(1, n_lane_tiles):
                mc = jnp.maximum(mc, parts[t])
            m_prev = m_ref[rows]
            m_new = jnp.maximum(m_prev, jnp.max(mc, axis=1, keepdims=True))
            alpha = jnp.exp(m_prev - m_new)
            ps = [jnp.exp(pt - m_new) for pt in parts]
            row_sum = ps[0]
            for t in range(1, n_lane_tiles):
                row_sum = row_sum + ps[t]
            l_ref[rows] = alpha * l_ref[rows] + jnp.sum(row_sum, axis=1, keepdims=True)
            m_ref[rows] = m_new
            al_ref[rows] = alpha
            for t in range(n_lane_tiles):
                p_ref[rows, t * LANES:(t + 1) * LANES] = ps[t].astype(BF16)
        k0 = pl.multiple_of(j * SLC_TILE, SLC_TILE)
        acc_ref[...] = al_ref[...] * acc_ref[...] + _dot(p_ref[...], vs_ref[pl.ds(k0, SLC_TILE), :])

    scores(0, s0_ref)

    def slc_pair(i, carry):
        ja = 2 * i
        jb = ja + 1

        @pl.when(ja >= j_near)
        def _():
            add_bias(ja, s0_ref)

        scores(jb, s1_ref)
        softmax_pv(ja, s0_ref)

        @pl.when((jb < n_tiles) & (jb >= j_near))
        def _():
            add_bias(jb, s1_ref)

        @pl.when(jb < n_tiles)
        def _():
            scores(jb + 1, s0_ref)
            softmax_pv(jb, s1_ref)

        return carry

    lax.fori_loop(0, (n_tiles + 1) // 2, slc_pair, 0)
    o_slc = acc_ref[...] / l_ref[...]

    gt = _sigmoid(gate_ref[...])
    o_cmp = oc_ref[...]
    o_win = ow_ref[...]
    outs = []
    for h in range(HPG):
        rows = slice(h * QB, (h + 1) * QB)
        outs.append(gt[:, 3 * h:3 * h + 1] * o_cmp[rows] + gt[:, 3 * h + 1:3 * h + 2] * o_slc[rows]
                    + gt[:, 3 * h + 2:3 * h + 3] * o_win[rows])
    o_ref[...] = jnp.concatenate(outs, axis=1).astype(o_ref.dtype)


def nsa_attention(q, kv, kvc, gates, toep, cmpb, B, T):
    N = q.shape[0]
    G, HPG, dh, QB = NSA_GROUPS, NSA_HPG, HEAD_DIM, Q_BLOCK
    nq = T // QB
    n_cmp = T // CMP_STRIDE
    n_slc = T // SLC_BLOCK
    c = jnp.arange(n_cmp)[:, None] * CMP_STRIDE
    sblk = jnp.arange(n_slc)[None, :] * SLC_BLOCK
    ovl = ((c < sblk + SLC_BLOCK) & (c + CMP_BLOCK > sblk)).astype(BF16)

    assert n_slc <= LANES
    member = (jnp.arange(T)[:, None] // SLC_BLOCK == jnp.arange(LANES)[None, :]).astype(BF16)
    k_slc = kv[:, :, 2 * G * dh:3 * G * dh].reshape(B, T, G, dh).transpose(0, 2, 1, 3)
    ksa = jnp.concatenate([k_slc, jnp.broadcast_to(member, (B, G, T, LANES))], axis=-1)

    def kv_spec(branch, which):
        col = (branch * 2 + which) * G
        return pl.BlockSpec((None, T, dh), lambda b, g, qi: (b, 0, col + g))

    def cmp_spec(which):
        return pl.BlockSpec((None, None, None, n_cmp, dh), lambda b, g, qi: (which, b, g, 0, 0))

    R = HPG * QB
    return pl.pallas_call(
        _nsa_attn_kernel,
        grid=(B, G, nq),
        in_specs=[pl.BlockSpec((QB, HPG * dh), lambda b, g, qi: (b * nq + qi, g)),
                  cmp_spec(0), cmp_spec(1),
                  pl.BlockSpec((None, None, T, dh + LANES), lambda b, g, qi: (b, g, 0, 0)),
                  kv_spec(1, 1), kv_spec(2, 0), kv_spec(2, 1),
                  pl.BlockSpec((QB, LANES), lambda b, g, qi: (b * nq + qi, g)),
                  pl.BlockSpec((None, N_TOEP, R, QB), lambda b, g, qi: (g, 0, 0, 0)),
                  pl.BlockSpec((None, R, n_cmp), lambda b, g, qi: (g, 0, 0)),
                  pl.BlockSpec((n_cmp, n_slc), lambda b, g, qi: (0, 0))],
        out_specs=pl.BlockSpec((QB, HPG * dh), lambda b, g, qi: (b * nq + qi, g)),
        out_shape=jax.ShapeDtypeStruct((N, G * HPG * dh), BF16),
        scratch_shapes=[pltpu.VMEM((R, dh + LANES), BF16),
                        pltpu.VMEM((R, SLC_TILE), F32),
                        pltpu.VMEM((R, SLC_TILE), F32),
                        pltpu.VMEM((R, SLC_TILE), BF16),
                        pltpu.VMEM((R, LANES), F32),
                        pltpu.VMEM((R, dh), F32),
                        pltpu.VMEM((R, dh), F32),
                        pltpu.VMEM((R, dh), F32), pltpu.VMEM((R, LANES), F32), pltpu.VMEM((R, LANES), F32)],
        compiler_params=_cparams(("arbitrary", "arbitrary", "arbitrary"), vmem_mib=56), name="nsa_attention",
    )(q, kvc, kvc, ksa, kv, kv, kv, gates, toep, cmpb, ovl)


def nsa_mixer(h, w_in, cmp_pos, w1k, w2k, w1v, w2v, w_o, toep, cmpb, B, T):
    N, D = h.shape
    G, HPG, dh = NSA_GROUPS, NSA_HPG, HEAD_DIM
    q_dim = NSA_HEADS * dh
    kv_dim = 3 * 2 * G * dh
    wq = w_in[:, :q_dim].astype(BF16)
    wkv = w_in[:, q_dim:q_dim + kv_dim].astype(BF16)
    wg = w_in[:, q_dim + kv_dim:].astype(BF16).reshape(D, G, HPG * 3)
    wg = jnp.zeros((D, G, LANES), BF16).at[:, :, :HPG * 3].set(wg).reshape(D, G * LANES)
    q = matmul(h, wq, BF16, scale=dh ** -0.5, name="nsa_in_q")
    kv = matmul(h, wkv, BF16, name="nsa_in_kv")
    gates = matmul(h, wg, F32, name="nsa_in_gate")
    kv3 = kv.reshape(B, T, kv_dim)
    kvc = nsa_compress(kv3[:, :, :2 * G * dh], cmp_pos, w1k, w2k, w1v, w2v, B, T)
    o = nsa_attention(q, kv3, kvc, gates, toep, cmpb, B, T)
    return matmul(o, w_o.astype(BF16), F32, name="nsa_out")


def kernel(x, c, rel_bias, ada_w, ada_b, ln_g, ln_b, nsa_w_in, nsa_cmp_pos, nsa_cmp_w1k, nsa_cmp_w2k, nsa_cmp_w1v, nsa_cmp_w2v, nsa_w_o, conv_w_in, conv_dw, conv_ln_g, conv_ln_b, conv_w_out, gdn_w_in, gdn_conv, gdn_a_log, gdn_dt_bias, gdn_norm_w, gdn_w_out, ffn_w1, ffn_w3, ffn_w2, moe_router, moe_w1, moe_w3, moe_w2):
    B, T, D = x.shape
    N = B * T
    depth = ada_w.shape[0]
    mod = adaln_all(c, ada_w, ada_b).reshape(depth, B, 6, 1, D)
    toep, cmpb = _nsa_bias_tables(rel_bias, T // CMP_STRIDE)
    x2 = x.reshape(N, D)

    def h_dtype_for(layer, sub):
        return F32 if (sub == 1 and layer % 2 == 1) else BF16

    h = modulate(x2, mod[0, :, 1], mod[0, :, 0], T, h_dtype_for(0, 0))
    for i in range(depth):
        sh1, sc1, g1, sh2, sc2, g2 = (mod[i, :, k] for k in range(6))
        kind, j = i % 3, i // 3
        if kind == 0:
            y = nsa_mixer(h, nsa_w_in[j], nsa_cmp_pos[j], nsa_cmp_w1k[j], nsa_cmp_w2k[j], nsa_cmp_w1v[j],
                          nsa_cmp_w2v[j], nsa_w_o[j], toep, cmpb, B, T)
        elif kind == 1:
            y = conformer_mixer(h, conv_w_in[j], conv_dw[j], conv_ln_g[j], conv_ln_b[j], conv_w_out[j], T)
        else:
            y = gdn_mixer(h, gdn_w_in[j], gdn_conv[j], gdn_a_log[j], gdn_dt_bias[j], gdn_norm_w[j],
                          gdn_w_out[j], B, T)
        lg = ln_g[i].reshape(2, 1, D)
        lb = ln_b[i].reshape(2, 1, D)
        x2, h = norm_mod(x2, y, g1, lg[0], lb[0], sc2, sh2, T, h_dtype_for(i, 1))
        m = i // 2
        last = i == depth - 1
        sc_n = None if last else mod[i + 1, :, 1]
        sh_n = None if last else mod[i + 1, :, 0]
        if i % 2 == 0:
            y = ffn_swiglu(h, ffn_w1[m], ffn_w3[m], ffn_w2[m])
            x2, h = norm_mod(x2, y, g2, lg[1], lb[1], sc_n, sh_n, T, h_dtype_for(i + 1, 0))
        else:
            x2, h = moe_sublayer(h, x2, moe_router[m], moe_w1[m], moe_w3[m], moe_w2[m],
                                 g2, lg[1], lb[1], sc_n, sh_n, T, h_dtype_for(i + 1, 0))
    return x2.reshape(B, T, D)
```

```python
import functools
import math

import jax
import jax.numpy as jnp
from jax import lax
from jax.experimental import pallas as pl
from jax.experimental.pallas import tpu as pltpu

F32 = jnp.float32
BF16 = jnp.bfloat16

DEPTH = 4
ALPHA = (2 * DEPTH) ** 0.25
LN_EPS = 1e-5
NEG_INF = -1e30
LOG2E = math.log2(math.e)

REL_BUCKETS = 32
REL_EXACT = 16
REL_MAX_DIST = 1024

NSA_HEADS = 16
NSA_GROUPS = 4
NSA_HPG = NSA_HEADS // NSA_GROUPS
HEAD_DIM = 128
CMP_BLOCK = 32
CMP_STRIDE = 16
CMP_HIDDEN = 256
SLC_BLOCK = 64
SLC_TOP_N = 16
WINDOW = 512
Q_BLOCK = 128
FORCE_SCORE = 1e6

CONV_WIDTH = 31
GDN_QK_HEADS = 16
GDN_V_HEADS = 32
GDN_CONV = 4
GDN_CHUNK = 64

N_EXPERTS = 8
TOP_K = 2

LANES = 128
SUBLANES = 8
MIB = 1 << 20


def _cparams(sem, vmem_mib=48):
    return pltpu.CompilerParams(dimension_semantics=sem, vmem_limit_bytes=vmem_mib * MIB)


def _dot(a, b):
    return jnp.dot(a, b, preferred_element_type=F32)


def _dot_nt(a, b):
    return lax.dot_general(a, b, (((1,), (1,)), ((), ())), preferred_element_type=F32)


def _split3(a):
    hi = a.astype(BF16)
    r1 = a - hi.astype(F32)
    mid = r1.astype(BF16)
    lo = (r1 - mid.astype(F32)).astype(BF16)
    return hi, mid, lo


def _dot_x3(a, b_bf16):
    hi, mid, lo = _split3(a)
    return _dot(hi, b_bf16) + _dot(mid, b_bf16) + _dot(lo, b_bf16)


def _silu(x):
    return x * (1.0 / (1.0 + jnp.exp(-x)))


def _sigmoid(x):
    return 1.0 / (1.0 + jnp.exp(-x))


def _softplus(x):
    return jnp.maximum(x, 0.0) + jnp.log(1.0 + jnp.exp(-jnp.abs(x)))


def _shr(x, pow2):
    return lax.shift_right_logical(x, jnp.full(x.shape, int(math.log2(pow2)), jnp.int32))


def _adaln_kernel(ct_ref, w_ref, b_ref, o_ref):
    w = w_ref[...]
    tn = w.shape[1]
    rows = []
    for b in range(ct_ref.shape[0]):
        cb = _silu(ct_ref[b])
        parts = [jnp.sum(w[:, j * LANES:(j + 1) * LANES] * cb, axis=0, keepdims=True)
                 for j in range(tn // LANES)]
        rows.append(jnp.concatenate(parts, axis=1))
    o_ref[...] = jnp.concatenate(rows, axis=0) + b_ref[...]


def adaln_all(c, ada_w, ada_b):
    B, D = c.shape
    depth, _, n_out = ada_w.shape
    tn = 1024
    ct = jnp.broadcast_to(c[:, :, None], (B, D, LANES))
    return pl.pallas_call(
        _adaln_kernel,
        grid=(depth, n_out // tn),
        in_specs=[pl.BlockSpec((B, D, LANES), lambda i, j: (0, 0, 0)),
                  pl.BlockSpec((None, D, tn), lambda i, j: (i, 0, j)),
                  pl.BlockSpec((None, 1, tn), lambda i, j: (i, 0, j))],
        out_specs=pl.BlockSpec((None, B, tn), lambda i, j: (i, 0, j)),
        out_shape=jax.ShapeDtypeStruct((depth, B, n_out), F32),
        compiler_params=_cparams(("arbitrary", "arbitrary")),
        name="adaln",
    )(ct, ada_w, ada_b.reshape(depth, 1, n_out))


def _mod_kernel(x_ref, sc_ref, sh_ref, h_ref):
    h_ref[...] = (x_ref[...] * (1.0 + sc_ref[...]) + sh_ref[...]).astype(h_ref.dtype)


def modulate(x2, sc, sh, T, h_dtype):
    N, D = x2.shape
    tm = 512
    vec = pl.BlockSpec((None, 1, D), lambda i: ((i * tm) // T, 0, 0))
    return pl.pallas_call(
        _mod_kernel,
        grid=(N // tm,),
        in_specs=[pl.BlockSpec((tm, D), lambda i: (i, 0)), vec, vec],
        out_specs=pl.BlockSpec((tm, D), lambda i: (i, 0)),
        out_shape=jax.ShapeDtypeStruct((N, D), h_dtype),
        compiler_params=_cparams(("arbitrary",)),
        name="modulate",
    )(x2, sc, sh)


def _deepnorm(x, y, gate, lg, lb):
    z = ALPHA * x + (1.0 + gate) * y
    mu = jnp.mean(z, axis=-1, keepdims=True)
    zc = z - mu
    var = jnp.mean(zc * zc, axis=-1, keepdims=True)
    return zc * lax.rsqrt(var + LN_EPS) * lg + lb


def _norm_mod_kernel(x_ref, y_ref, g_ref, lg_ref, lb_ref, sc_ref, sh_ref, xo_ref, ho_ref):
    xn = _deepnorm(x_ref[...], y_ref[...].astype(F32), g_ref[...], lg_ref[...], lb_ref[...])
    xo_ref[...] = xn
    ho_ref[...] = (xn * (1.0 + sc_ref[...]) + sh_ref[...]).astype(ho_ref.dtype)


def _norm_kernel(x_ref, y_ref, g_ref, lg_ref, lb_ref, xo_ref):
    xo_ref[...] = _deepnorm(x_ref[...], y_ref[...].astype(F32), g_ref[...], lg_ref[...], lb_ref[...])


def norm_mod(x2, y, gate, lg, lb, sc, sh, T, h_dtype):
    N, D = x2.shape
    tm = 256
    row = pl.BlockSpec((tm, D), lambda i: (i, 0))
    vec = pl.BlockSpec((None, 1, D), lambda i: ((i * tm) // T, 0, 0))
    par = pl.BlockSpec((1, D), lambda i: (0, 0))
    if sc is None:
        return pl.pallas_call(
            _norm_kernel, grid=(N // tm,),
            in_specs=[row, row, vec, par, par], out_specs=row,
            out_shape=jax.ShapeDtypeStruct((N, D), F32),
            compiler_params=_cparams(("arbitrary",)), name="deepnorm",
        )(x2, y, gate, lg, lb), None
    return pl.pallas_call(
        _norm_mod_kernel, grid=(N // tm,),
        in_specs=[row, row, vec, par, par, vec, vec], out_specs=[row, row],
        out_shape=[jax.ShapeDtypeStruct((N, D), F32), jax.ShapeDtypeStruct((N, D), h_dtype)],
        compiler_params=_cparams(("arbitrary",)), name="deepnorm_mod",
    )(x2, y, gate, lg, lb, sc, sh)


def _mm_kernel(x_ref, w_ref, o_ref, *, scale):
    acc = _dot(x_ref[...], w_ref[...])
    if scale != 1.0:
        acc = acc * scale
    o_ref[...] = acc.astype(o_ref.dtype)


def matmul(x, w, out_dtype, *, tm=1024, tn=512, scale=1.0, name="matmul"):
    M, K = x.shape
    N = w.shape[1]
    tn = min(tn, N)
    return pl.pallas_call(
        functools.partial(_mm_kernel, scale=scale),
        grid=(M // tm, N // tn),
        in_specs=[pl.BlockSpec((tm, K), lambda i, j: (i, 0)),
                  pl.BlockSpec((K, tn), lambda i, j: (0, j))],
        out_specs=pl.BlockSpec((tm, tn), lambda i, j: (i, j)),
        out_shape=jax.ShapeDtypeStruct((M, N), out_dtype),
        compiler_params=_cparams(("arbitrary", "arbitrary")),
        name=name,
    )(x, w)


def _mm_glu_kernel(x_ref, wa_ref, wb_ref, o_ref):
    x = x_ref[...]
    a = _dot(x, wa_ref[...])
    b = _dot(x, wb_ref[...])
    o_ref[...] = (a * _sigmoid(b)).astype(o_ref.dtype)


def matmul_glu(x, w, out_dtype, *, tm=1024, tn=512):
    M, K = x.shape
    n = w.shape[1] // 2
    nj = n // tn
    return pl.pallas_call(
        _mm_glu_kernel,
        grid=(M // tm, nj),
        in_specs=[pl.BlockSpec((tm, K), lambda i, j: (i, 0)),
                  pl.BlockSpec((K, tn), lambda i, j: (0, j)),
                  pl.BlockSpec((K, tn), lambda i, j: (0, j + nj))],
        out_specs=pl.BlockSpec((tm, tn), lambda i, j: (i, j)),
        out_shape=jax.ShapeDtypeStruct((M, n), out_dtype),
        compiler_params=_cparams(("arbitrary", "arbitrary")),
        name="matmul_glu",
    )(x, w, w)


def _ffn_kernel(x_ref, w1_ref, w3_ref, w2_ref, o_ref):
    f = pl.program_id(1)
    @pl.when(f == 0)
    def _():
        o_ref[...] = jnp.zeros_like(o_ref)

    x = x_ref[...]
    a = _dot(x, w1_ref[...].astype(BF16))
    b = _dot(x, w3_ref[...].astype(BF16))
    o_ref[...] += _dot((_silu(a) * b).astype(BF16), w2_ref[...].astype(BF16))


def ffn_swiglu(h, w1, w3, w2, layer, *, tm=1024, tf=256):
    N, D = h.shape
    Fd = w1.shape[2]
    single = pl.Buffered(1)
    return pl.pallas_call(
        _ffn_kernel,
        grid=(N // tm, Fd // tf),
        in_specs=[pl.BlockSpec((tm, D), lambda i, f: (i, 0), pipeline_mode=single),
                  pl.BlockSpec((None, D, tf), lambda i, f: (layer, 0, f)),
                  pl.BlockSpec((None, D, tf), lambda i, f: (layer, 0, f)),
                  pl.BlockSpec((None, tf, D), lambda i, f: (layer, f, 0))],
        out_specs=pl.BlockSpec((tm, D), lambda i, f: (i, 0), pipeline_mode=single),
        out_shape=jax.ShapeDtypeStruct((N, D), F32),
        compiler_params=_cparams(("arbitrary", "arbitrary")),
        name="ffn_swiglu",
    )(h, w1, w3, w2)


def _router_kernel(h_ref, rh_ref, rl_ref, idx_ref, wt_ref):
    h = h_ref[...]
    hh = h.astype(BF16)
    hl = (h - hh.astype(F32)).astype(BF16)
    rh = rh_ref[...]
    logits = _dot(hh, rh) + _dot(hl, rh) + _dot(hh, rl_ref[...])
    lane = lax.broadcasted_iota(jnp.int32, logits.shape, 1)
    lanef = lane.astype(F32)
    logits = jnp.where(lane < N_EXPERTS, logits, -jnp.inf)
    m1 = jnp.max(logits, axis=1, keepdims=True)
    i1 = jnp.min(jnp.where(logits == m1, lanef, float(LANES)), axis=1, keepdims=True)
    rest = jnp.where(lanef == i1, -jnp.inf, logits)
    m2 = jnp.max(rest, axis=1, keepdims=True)
    i2 = jnp.min(jnp.where(rest == m2, lanef, float(LANES)), axis=1, keepdims=True)
    e2 = jnp.exp(m2 - m1)
    w1 = 1.0 / (1.0 + e2)
    w2 = e2 / (1.0 + e2)
    idx_ref[...] = jnp.where(lane == 0, i1, jnp.where(lane == 1, i2, 0.0)).astype(jnp.int32)
    wt_ref[...] = jnp.where(lane == 0, w1, jnp.where(lane == 1, w2, 0.0))


def moe_router(h, router):
    N, D = h.shape
    tm = 512
    rp = jnp.zeros((D, LANES), F32).at[:, :N_EXPERTS].set(router)
    rh = rp.astype(BF16)
    rl = (rp - rh.astype(F32)).astype(BF16)
    row = pl.BlockSpec((tm, LANES), lambda i: (i, 0))
    return pl.pallas_call(
        _router_kernel, grid=(N // tm,),
        in_specs=[pl.BlockSpec((tm, D), lambda i: (i, 0)),
                  pl.BlockSpec((D, LANES), lambda i: (0, 0)),
                  pl.BlockSpec((D, LANES), lambda i: (0, 0))],
        out_specs=[row, row],
        out_shape=[jax.ShapeDtypeStruct((N, LANES), jnp.int32), jax.ShapeDtypeStruct((N, LANES), F32)],
        compiler_params=_cparams(("arbitrary",)), name="moe_router",
    )(h, rh, rl)


def _row_copy(src_hbm, dst_ref, sem, src_row, dst_row):
    return pltpu.make_async_copy(src_hbm.at[pl.ds(src_row, 1), :], dst_ref.at[pl.ds(dst_row, 1), :], sem)


def _row_put(src_ref, dst_hbm, sem, src_row, dst_row):
    return pltpu.make_async_copy(src_ref.at[pl.ds(src_row, 1), :], dst_hbm.at[pl.ds(dst_row, 1), :], sem)


def _dispatch_rows_kernel(pos_ref, x_ref, init_hbm, o_hbm, sem, *, rows):
    del init_hbm
    base = pl.program_id(0) * rows

    def start(r, carry):
        for k in range(TOP_K):
            _row_put(x_ref, o_hbm, sem, r, pos_ref[TOP_K * (base + r) + k]).start()
        return carry

    def wait(r, carry):
        for k in range(TOP_K):
            _row_put(x_ref, o_hbm, sem, r, pos_ref[TOP_K * (base + r) + k]).wait()
        return carry

    lax.fori_loop(0, rows, start, 0)
    lax.fori_loop(0, rows, wait, 0)


def dispatch_rows(x, pos, n_out, *, rows=256):
    N, D = x.shape
    return pl.pallas_call(
        functools.partial(_dispatch_rows_kernel, rows=rows),
        grid_spec=pltpu.PrefetchScalarGridSpec(
            num_scalar_prefetch=1, grid=(N // rows,),
            in_specs=[pl.BlockSpec((rows, D), lambda i, p: (i, 0)), pl.BlockSpec(memory_space=pl.ANY)],
            out_specs=pl.BlockSpec(memory_space=pl.ANY),
            scratch_shapes=[pltpu.SemaphoreType.DMA(())]),
        out_shape=jax.ShapeDtypeStruct((n_out, D), x.dtype),
        input_output_aliases={2: 0},
        compiler_params=_cparams(("arbitrary",)), name="moe_dispatch",
    )(pos, x, jnp.zeros((n_out, D), x.dtype))


def _moe_mm_kernel(te_ref, tv_ref, x_ref, w1_ref, w3_ref, w2_ref, o_ref, xb_ref):
    i = pl.program_id(0)
    f = pl.program_id(1)

    @pl.when(f == 0)
    def _():
        xb_ref[...] = x_ref[...].astype(BF16)
        o_ref[...] = jnp.zeros_like(o_ref)

    @pl.when(tv_ref[i] > 0)
    def _():
        x = xb_ref[...]
        a = _dot(x, w1_ref[...].astype(BF16))
        b = _dot(x, w3_ref[...].astype(BF16))
        o_ref[...] += _dot((_silu(a) * b).astype(BF16), w2_ref[...].astype(BF16))


def moe_grouped_swiglu(xs, tile_expert, tile_valid, w1, w3, w2, layer, *, tm, tf=256):
    R, D = xs.shape
    Fd = w1.shape[3]
    single = pl.Buffered(1)
    return pl.pallas_call(
        _moe_mm_kernel,
        grid_spec=pltpu.PrefetchScalarGridSpec(
            num_scalar_prefetch=2, grid=(R // tm, Fd // tf),
            in_specs=[pl.BlockSpec((tm, D), lambda i, f, te, tv: (i, 0), pipeline_mode=single),
                      pl.BlockSpec((None, None, D, tf), lambda i, f, te, tv: (layer, te[i], 0, f * tv[i])),
                      pl.BlockSpec((None, None, D, tf), lambda i, f, te, tv: (layer, te[i], 0, f * tv[i])),
                      pl.BlockSpec((None, None, tf, D), lambda i, f, te, tv: (layer, te[i], f * tv[i], 0))],
            out_specs=pl.BlockSpec((tm, D), lambda i, f, te, tv: (i, 0), pipeline_mode=single),
            scratch_shapes=[pltpu.VMEM((tm, D), BF16)]),
        out_shape=jax.ShapeDtypeStruct((R, D), F32),
        compiler_params=_cparams(("arbitrary", "arbitrary"), vmem_mib=52), name="moe_grouped_swiglu",
    )(tile_expert, tile_valid, xs, w1, w3, w2)


def _combine_gather(pos_ref, ys_hbm, buf_ref, sem, base, rows):
    def start(r, carry):
        _row_copy(ys_hbm, buf_ref.at[0], sem, pos_ref[2 * (base + r)], r).start()
        _row_copy(ys_hbm, buf_ref.at[1], sem, pos_ref[2 * (base + r) + 1], r).start()
        return carry

    def wait(r, carry):
        _row_copy(ys_hbm, buf_ref.at[0], sem, pos_ref[2 * (base + r)], r).wait()
        _row_copy(ys_hbm, buf_ref.at[1], sem, pos_ref[2 * (base + r) + 1], r).wait()
        return carry

    lax.fori_loop(0, rows, start, 0)
    lax.fori_loop(0, rows, wait, 0)


def _moe_combine(wt_ref, buf_ref):
    wt = wt_ref[...]
    return wt[:, 0:1] * buf_ref[0] + wt[:, 1:2] * buf_ref[1]


def _combine_norm_mod_kernel(pos_ref, ys_hbm, wt_ref, x_ref, g_ref, lg_ref, lb_ref, sc_ref, sh_ref,
                             xo_ref, ho_ref, buf_ref, sem, *, rows):
    _combine_gather(pos_ref, ys_hbm, buf_ref, sem, pl.program_id(0) * rows, rows)
    xn = _deepnorm(x_ref[...], _moe_combine(wt_ref, buf_ref), g_ref[...], lg_ref[...], lb_ref[...])
    xo_ref[...] = xn
    ho_ref[...] = (xn * (1.0 + sc_ref[...]) + sh_ref[...]).astype(ho_ref.dtype)


def _combine_norm_kernel(pos_ref, ys_hbm, wt_ref, x_ref, g_ref, lg_ref, lb_ref, xo_ref, buf_ref, sem, *, rows):
    _combine_gather(pos_ref, ys_hbm, buf_ref, sem, pl.program_id(0) * rows, rows)
    xo_ref[...] = _deepnorm(x_ref[...], _moe_combine(wt_ref, buf_ref), g_ref[...], lg_ref[...], lb_ref[...])


def moe_combine_norm_mod(ys, pos, wts, x2, gate, lg, lb, sc, sh, T, h_dtype, *, rows=256):
    N, D = x2.shape
    row = pl.BlockSpec((rows, D), lambda i, p: (i, 0))
    vec = pl.BlockSpec((None, 1, D), lambda i, p: ((i * rows) // T, 0, 0))
    par = pl.BlockSpec((1, D), lambda i, p: (0, 0))
    wsp = pl.BlockSpec((rows, LANES), lambda i, p: (i, 0))
    anysp = pl.BlockSpec(memory_space=pl.ANY)
    scratch = [pltpu.VMEM((2, rows, D), F32), pltpu.SemaphoreType.DMA(())]
    if sc is None:
        out = pl.pallas_call(
            functools.partial(_combine_norm_kernel, rows=rows),
            grid_spec=pltpu.PrefetchScalarGridSpec(
                num_scalar_prefetch=1, grid=(N // rows,),
                in_specs=[anysp, wsp, row, vec, par, par], out_specs=row, scratch_shapes=scratch),
            out_shape=jax.ShapeDtypeStruct((N, D), F32),
            compiler_params=_cparams(("arbitrary",)), name="moe_combine_norm",
        )(pos, ys, wts, x2, gate, lg, lb)
        return out, None
    return pl.pallas_call(
        functools.partial(_combine_norm_mod_kernel, rows=rows),
        grid_spec=pltpu.PrefetchScalarGridSpec(
            num_scalar_prefetch=1, grid=(N // rows,),
            in_specs=[anysp, wsp, row, vec, par, par, vec, vec], out_specs=[row, row],
            scratch_shapes=scratch),
        out_shape=[jax.ShapeDtypeStruct((N, D), F32), jax.ShapeDtypeStruct((N, D), h_dtype)],
        compiler_params=_cparams(("arbitrary",)), name="moe_combine_norm_mod",
    )(pos, ys, wts, x2, gate, lg, lb, sc, sh)


def _route_tables(idx2, tm, n_tiles):
    e_flat = idx2.reshape(-1)
    onehot = (jnp.arange(N_EXPERTS, dtype=jnp.int32)[:, None] == e_flat[None, :]).astype(jnp.int32)
    csum = jnp.cumsum(onehot, axis=1)
    counts = csum[:, -1]
    rank = jnp.sum(onehot * (csum - 1), axis=0)
    tiles_per = (counts + tm - 1) // tm
    tile_end = jnp.cumsum(tiles_per)
    pad_start = (tile_end - tiles_per) * tm
    t = jnp.arange(n_tiles)
    te = jnp.sum((t[:, None] >= tile_end[None, :]).astype(jnp.int32), axis=1)
    tile_valid = (te < N_EXPERTS).astype(jnp.int32)
    tile_expert = jnp.minimum(te, N_EXPERTS - 1).astype(jnp.int32)
    pos = (jnp.sum(onehot * pad_start[:, None], axis=0) + rank).astype(jnp.int32)
    return tile_expert, tile_valid, pos


def moe_sublayer(h, x2, router, w1, w3, w2, layer, gate, lg, lb, sc, sh, T, h_dtype, *, tm=1024):
    N, D = h.shape
    idx, wts = moe_router(h, router)
    n_tiles = (N * TOP_K) // tm + N_EXPERTS
    tile_expert, tile_valid, pos = _route_tables(idx[:, :TOP_K], tm, n_tiles)
    xs = dispatch_rows(h, pos, n_tiles * tm)
    ys = moe_grouped_swiglu(xs, tile_expert, tile_valid, w1, w3, w2, layer, tm=tm)
    return moe_combine_norm_mod(ys, pos, wts, x2, gate, lg, lb, sc, sh, T, h_dtype)


DW_HALO = 32
DW_ROWS = 128
DW_PITCH = 3
DW_GROUP = 4


def _dwconv_kernel(u_ref, halo_ref, dw_ref, lg_ref, lb_ref, o_ref, ext_ref, acc_ref, *, tiles_per_seq):
    i = pl.program_id(0)
    first = (i % tiles_per_seq) == 0
    lead = DW_HALO - (CONV_WIDTH - 1)
    P = DW_PITCH
    lg = lg_ref[...]
    lb = lb_ref[...]
    for sl in range(u_ref.shape[1] // LANES):
        cols = slice(sl * LANES, (sl + 1) * LANES)
        ext_ref[sl, pl.ds(0, DW_HALO, stride=P), :] = jnp.where(first, 0.0, halo_ref[:, cols].astype(F32))
        ext_ref[sl, pl.ds(P * DW_HALO, DW_ROWS, stride=P), :] = u_ref[:, cols].astype(F32)
    for sl in range(u_ref.shape[1] // LANES):
        cols = slice(sl * LANES, (sl + 1) * LANES)
        for rg in range(DW_ROWS // (SUBLANES * DW_GROUP)):
            accs = [None] * DW_GROUP
            for k in range(CONV_WIDTH):
                w = dw_ref[k:k + 1, cols]
                for t in range(DW_GROUP):
                    r0 = (rg * DW_GROUP + t) * SUBLANES
                    term = ext_ref[sl, pl.ds(P * (r0 + lead + k), SUBLANES, stride=P), :] * w
                    accs[t] = term if accs[t] is None else accs[t] + term
            for t in range(DW_GROUP):
                r0 = (rg * DW_GROUP + t) * SUBLANES
                acc_ref[r0:r0 + SUBLANES, cols] = accs[t]
    pack_rows = 2 * SUBLANES
    for rc in range(DW_ROWS // pack_rows):
        r0 = rc * pack_rows
        acc = acc_ref[r0:r0 + pack_rows, :]
        mu = jnp.mean(acc, axis=-1, keepdims=True)
        zc = acc - mu
        var = jnp.mean(zc * zc, axis=-1, keepdims=True)
        v = zc * lax.rsqrt(var + LN_EPS) * lg + lb
        o_ref[r0:r0 + pack_rows, :] = _silu(v).astype(o_ref.dtype)


def dwconv_ln_silu(u, dw, lg, lb, T):
    N, D = u.shape
    hb = DW_ROWS // DW_HALO
    return pl.pallas_call(
        functools.partial(_dwconv_kernel, tiles_per_seq=T // DW_ROWS),
        grid=(N // DW_ROWS,),
        in_specs=[pl.BlockSpec((DW_ROWS, D), lambda i: (i, 0)),
                  pl.BlockSpec((DW_HALO, D), lambda i: (jnp.maximum(i * hb - 1, 0), 0)),
                  pl.BlockSpec((CONV_WIDTH, D), lambda i: (0, 0)),
                  pl.BlockSpec((1, D), lambda i: (0, 0)),
                  pl.BlockSpec((1, D), lambda i: (0, 0))],
        out_specs=pl.BlockSpec((DW_ROWS, D), lambda i: (i, 0)),
        out_shape=jax.ShapeDtypeStruct((N, D), BF16),
        scratch_shapes=[pltpu.VMEM((D // LANES, DW_PITCH * (DW_HALO + DW_ROWS), LANES), F32),
                        pltpu.VMEM((DW_ROWS, D), F32)],
        compiler_params=_cparams(("arbitrary",)), name="dwconv_ln_silu",
    )(u, u, dw, lg.reshape(1, D), lb.reshape(1, D))


def conformer_mixer(h, w_in, dw, lg, lb, w_out, T):
    u = matmul_glu(h, w_in.astype(BF16), BF16)
    v = dwconv_ln_silu(u, dw, lg, lb, T)
    return matmul(v, w_out.astype(BF16), F32, name="conv_out")


GC_HALO = 16
GC_ROWS = 512
GC_COLS = 1024


def _gconv_kernel(x_ref, halo_ref, w_ref, o_ref, ext_ref, *, tiles_per_seq):
    i = pl.program_id(0)
    first = (i % tiles_per_seq) == 0
    lead = GC_HALO - (GDN_CONV - 1)
    P = DW_PITCH
    pack_rows = 2 * SUBLANES
    for sl in range(GC_COLS // LANES):
        cols = slice(sl * LANES, (sl + 1) * LANES)
        ext_ref[sl, pl.ds(0, GC_HALO, stride=P), :] = jnp.where(first, 0.0, halo_ref[:, cols].astype(F32))
        ext_ref[sl, pl.ds(P * GC_HALO, GC_ROWS, stride=P), :] = x_ref[:, cols].astype(F32)
    for sl in range(GC_COLS // LANES):
        cols = slice(sl * LANES, (sl + 1) * LANES)
        taps = [w_ref[k:k + 1, cols] for k in range(GDN_CONV)]
        for rc in range(GC_ROWS // pack_rows):
            halves = []
            for r0 in (rc * pack_rows, rc * pack_rows + SUBLANES):
                acc = None
                for k in range(GDN_CONV):
                    term = ext_ref[sl, pl.ds(P * (r0 + lead + k), SUBLANES, stride=P), :] * taps[k]
                    acc = term if acc is None else acc + term
                halves.append(acc)
            out = jnp.concatenate(halves, axis=0)
            o_ref[rc * pack_rows:(rc + 1) * pack_rows, cols] = _silu(out).astype(o_ref.dtype)


def gdn_conv_silu(x, w, T):
    N, C = x.shape
    hb = GC_ROWS // GC_HALO
    return pl.pallas_call(
        functools.partial(_gconv_kernel, tiles_per_seq=T // GC_ROWS),
        grid=(N // GC_ROWS, C // GC_COLS),
        in_specs=[pl.BlockSpec((GC_ROWS, GC_COLS), lambda i, j: (i, j)),
                  pl.BlockSpec((GC_HALO, GC_COLS), lambda i, j: (jnp.maximum(i * hb - 1, 0), j)),
                  pl.BlockSpec((GDN_CONV, GC_COLS), lambda i, j: (0, j))],
        out_specs=pl.BlockSpec((GC_ROWS, GC_COLS), lambda i, j: (i, j)),
        out_shape=jax.ShapeDtypeStruct((N, C), BF16),
        scratch_shapes=[pltpu.VMEM((GC_COLS // LANES, DW_PITCH * (GC_HALO + GC_ROWS), LANES), F32)],
        compiler_params=_cparams(("arbitrary", "arbitrary")), name="gdn_conv_silu",
    )(x, x, w)


GP_ROWS = 512
GP_SUB = 256


def _l2norm(t):
    return t * lax.rsqrt(jnp.sum(t * t, axis=-1, keepdims=True) + 1e-6)


def _gdn_prep_kernel(q_ref, k_ref, v_ref, braw_ref, araw_ref, alog_ref, dtb_ref,
                     u_ref, w_ref, qg_ref, kdt_ref, attn_ref, gl_ref):
    C = GDN_CHUNK
    S = GP_SUB
    ri = lax.broadcasted_iota(jnp.int32, (S, S), 0)
    ci = lax.broadcasted_iota(jnp.int32, (S, S), 1)
    same = _shr(ri, C) == _shr(ci, C)
    lower = same & (ri >= ci)
    strict = same & (ri > ci)
    eye = ri == ci
    cum_mat = (same & (ri <= ci)).astype(BF16)
    tot_mat = same.astype(BF16)
    eye_f = eye.astype(F32)

    def to_col(row):
        return jnp.sum(jnp.where(eye, jnp.broadcast_to(row, (S, S)), 0.0), axis=1, keepdims=True)

    n_sb = GP_ROWS // S
    chains = [(sb, e) for sb in range(n_sb) for e in range(2)]
    qs, ks, grams, qks = [], [], [], []
    for sb in range(n_sb):
        rows = slice(sb * S, (sb + 1) * S)
        q = _l2norm(q_ref[rows, :].astype(F32)) * (HEAD_DIM ** -0.5)
        k = _l2norm(k_ref[rows, :].astype(F32))
        kb16 = k.astype(BF16)
        qs.append(q)
        ks.append(k)
        grams.append(_dot_nt(kb16, kb16))
        qks.append(_dot_nt(q.astype(BF16), kb16))

    gcums, gtots, betas, decays, invs, pws = [], [], [], [], [], []
    for sb, e in chains:
        rows = slice(sb * S, (sb + 1) * S)
        a_row = araw_ref[e, :, rows]
        b_row = braw_ref[e, :, rows]
        g_row = -jnp.exp(alog_ref[e, :, 0:1]) * _softplus(a_row + dtb_ref[e, :, 0:1])
        g8 = jnp.broadcast_to(g_row, (SUBLANES, S))
        gcum_row = _dot_x3(g8, cum_mat)[0:1, :]
        gcum = to_col(gcum_row)
        gtots.append(to_col(_dot_x3(g8, tot_mat)[0:1, :]))
        beta = to_col(_sigmoid(b_row))
        decay = jnp.exp(jnp.where(lower, gcum - gcum_row, NEG_INF))
        a = jnp.where(strict, beta * grams[sb] * decay, 0.0)
        gcums.append(gcum)
        betas.append(beta)
        decays.append(decay)
        invs.append(eye_f - a)
        pws.append(a)

    for _ in range(int(math.log2(C)) - 1):
        pw16 = [pw.astype(BF16) for pw in pws]
        pws = [_dot(p16, p16) for p16 in pw16]
        invs = [inv + _dot(inv.astype(BF16), pw.astype(BF16)) for inv, pw in zip(invs, pws)]

    for ci_, (sb, e) in enumerate(chains):
        rows = slice(sb * S, (sb + 1) * S)
        cols = slice(e * HEAD_DIM, (e + 1) * HEAD_DIM)
        q, k = qs[sb], ks[sb]
        gcum, gtot, beta = gcums[ci_], gtots[ci_], betas[ci_]
        eg = jnp.exp(gcum)
        v = v_ref[rows, cols].astype(F32)
        inv16 = invs[ci_].astype(BF16)
        u_ref[rows, cols] = _dot(inv16, (v * beta).astype(BF16))
        w_ref[rows, cols] = _dot(inv16, (k * (beta * eg)).astype(BF16)).astype(BF16)
        qg_ref[rows, cols] = (q * eg).astype(BF16)
        kdec = k * jnp.exp(gtot - gcum)
        kdt_ref[e, :, rows] = kdec.T.astype(BF16)
        attn = qks[sb] * decays[ci_]
        for c in range(S // C):
            attn_ref[e, sb * S + c * C:sb * S + (c + 1) * C, :] = (
                attn[c * C:(c + 1) * C, c * C:(c + 1) * C].astype(BF16))
            gl_ref[e, sb * (S // C) + c:sb * (S // C) + c + 1, :] = jnp.broadcast_to(
                jnp.exp(gtot[c * C:c * C + 1, :]), (1, LANES))


def gdn_prep(qkv, baT, a_log, dt_bias, B, T):
    N = qkv.shape[0]
    HK, HV, dh = GDN_QK_HEADS, GDN_V_HEADS, HEAD_DIM
    nt = T // GP_ROWS
    nc = GP_ROWS // GDN_CHUNK
    rowmap = lambda b, hk, t: (b * nt + t)
    alog = jnp.broadcast_to(a_log.astype(F32)[:, None, None], (HV, 1, LANES))
    dtb = jnp.broadcast_to(dt_bias.astype(F32)[:, None, None], (HV, 1, LANES))
    out_shape = [jax.ShapeDtypeStruct((N, HV * dh), F32),
                 jax.ShapeDtypeStruct((N, HV * dh), BF16),
                 jax.ShapeDtypeStruct((N, HV * dh), BF16),
                 jax.ShapeDtypeStruct((B, HV, dh, T), BF16),
                 jax.ShapeDtypeStruct((B, HV, T, GDN_CHUNK), BF16),
                 jax.ShapeDtypeStruct((B, HV, T // GDN_CHUNK, LANES), F32)]
    big = pl.BlockSpec((GP_ROWS, 2 * dh), lambda b, hk, t: (rowmap(b, hk, t), hk))
    return pl.pallas_call(
        _gdn_prep_kernel,
        grid=(B, HK, nt),
        in_specs=[pl.BlockSpec((GP_ROWS, dh), lambda b, hk, t: (rowmap(b, hk, t), hk)),
                  pl.BlockSpec((GP_ROWS, dh), lambda b, hk, t: (rowmap(b, hk, t), HK + hk)),
                  pl.BlockSpec((GP_ROWS, 2 * dh), lambda b, hk, t: (rowmap(b, hk, t), HK + hk)),
                  pl.BlockSpec((None, 2, 1, GP_ROWS), lambda b, hk, t: (b, hk, 0, t)),
                  pl.BlockSpec((None, 2, 1, GP_ROWS), lambda b, hk, t: (b, HK + hk, 0, t)),
                  pl.BlockSpec((2, 1, LANES), lambda b, hk, t: (hk, 0, 0)),
                  pl.BlockSpec((2, 1, LANES), lambda b, hk, t: (hk, 0, 0))],
        out_specs=[big, big, big,
                   pl.BlockSpec((None, 2, dh, GP_ROWS), lambda b, hk, t: (b, hk, 0, t)),
                   pl.BlockSpec((None, 2, GP_ROWS, GDN_CHUNK), lambda b, hk, t: (b, hk, t, 0)),
                   pl.BlockSpec((None, 2, nc, LANES), lambda b, hk, t: (b, hk, t, 0))],
        out_shape=out_shape,
        compiler_params=_cparams(("arbitrary", "arbitrary", "arbitrary")), name="gdn_prep",
    )(qkv, qkv, qkv, baT, baT, alog, dtb)


GS_HEADS = 4
GS_ROWS = 512


def _gdn_scan_kernel(u_ref, w_ref, qg_ref, kdt_ref, attn_ref, gl_ref, z_ref, nw_ref, o_ref, s_ref):
    C = GDN_CHUNK

    @pl.when(pl.program_id(2) == 0)
    def _():
        s_ref[...] = jnp.zeros_like(s_ref)

    nw = nw_ref[...]
    for c in range(GS_ROWS // C):
        rows = slice(c * C, (c + 1) * C)
        for hh in range(GS_HEADS):
            cols = slice(hh * HEAD_DIM, (hh + 1) * HEAD_DIM)
            s = s_ref[hh]
            lhs = jnp.concatenate([w_ref[rows, cols], qg_ref[rows, cols]], axis=0)
            r = _dot(lhs, s.astype(BF16))
            v_new = u_ref[rows, cols] - r[:C]
            lhs2 = jnp.concatenate([attn_ref[hh, rows, :], kdt_ref[hh, :, rows]], axis=0)
            r2 = _dot(lhs2, v_new.astype(BF16))
            o = r[C:] + r2[:C]
            s_ref[hh] = s * gl_ref[hh, c:c + 1, :] + r2[C:]
            z = z_ref[rows, cols].astype(F32)
            o = o * lax.rsqrt(jnp.mean(o * o, axis=-1, keepdims=True) + 1e-6) * nw * _silu(z)
            o_ref[rows, cols] = o.astype(o_ref.dtype)


def gdn_scan(u, w, qg, kdt, attn, gl, z, norm_w, B, T):
    N = u.shape[0]
    HV, dh = GDN_V_HEADS, HEAD_DIM
    nt = T // GS_ROWS
    nc = GS_ROWS // GDN_CHUNK
    wide = pl.BlockSpec((GS_ROWS, GS_HEADS * dh), lambda b, g, t: (b * nt + t, g))
    return pl.pallas_call(
        _gdn_scan_kernel,
        grid=(B, HV // GS_HEADS, nt),
        in_specs=[wide, wide, wide,
                  pl.BlockSpec((None, GS_HEADS, dh, GS_ROWS), lambda b, g, t: (b, g, 0, t)),
                  pl.BlockSpec((None, GS_HEADS, GS_ROWS, GDN_CHUNK), lambda b, g, t: (b, g, t, 0)),
                  pl.BlockSpec((None, GS_HEADS, nc, LANES), lambda b, g, t: (b, g, t, 0)),
                  wide,
                  pl.BlockSpec((1, dh), lambda b, g, t: (0, 0))],
        out_specs=wide,
        out_shape=jax.ShapeDtypeStruct((N, HV * dh), BF16),
        scratch_shapes=[pltpu.VMEM((GS_HEADS, dh, dh), F32)],
        compiler_params=_cparams(("arbitrary", "arbitrary", "arbitrary")), name="gdn_scan",
    )(u, w, qg, kdt, attn, gl, z, norm_w.astype(F32).reshape(1, dh))


def gdn_mixer(h, w_in, conv_w, a_log, dt_bias, norm_w, w_out, B, T):
    N, D = h.shape
    qkv_dim = 2 * GDN_QK_HEADS * HEAD_DIM + GDN_V_HEADS * HEAD_DIM
    v_dim = GDN_V_HEADS * HEAD_DIM
    w_qkv = w_in[:, :qkv_dim].astype(BF16)
    w_z = w_in[:, qkv_dim:qkv_dim + v_dim].astype(BF16)
    w_ba = jnp.zeros((D, LANES), BF16).at[:, :2 * GDN_V_HEADS].set(w_in[:, qkv_dim + v_dim:].astype(BF16))
    qkv = matmul(h, w_qkv, BF16, name="gdn_in_qkv")
    z = matmul(h, w_z, BF16, name="gdn_in_z")
    ba = matmul(h, w_ba, F32, name="gdn_in_ba")
    qkv = gdn_conv_silu(qkv, conv_w, T)
    baT = ba[:, :2 * GDN_V_HEADS].reshape(B, T, 2 * GDN_V_HEADS).transpose(0, 2, 1)[:, :, None, :]
    u, w, qg, kdt, attn, gl = gdn_prep(qkv, baT, a_log, dt_bias, B, T)
    o = gdn_scan(u, w, qg, kdt, attn, gl, z, norm_w, B, T)
    return matmul(o, w_out.astype(BF16), F32, name="gdn_out")


def _rel_bucket(dist):
    n = jnp.maximum(dist, 0)
    large = REL_EXACT + (jnp.log(jnp.maximum(n, 1).astype(F32) / REL_EXACT)
                         / math.log(REL_MAX_DIST / REL_EXACT) * (REL_BUCKETS - REL_EXACT)).astype(jnp.int32)
    return jnp.where(n < REL_EXACT, n, jnp.minimum(large, REL_BUCKETS - 1))


N_TOEP = 9
BIAS_PAD = 144


def _toeplitz(rows_rev):
    lead = rows_rev.shape[:-1]
    n = Q_BLOCK
    t = jnp.tile(rows_rev, (1,) * len(lead) + (n,))[..., :n * 2 * n].reshape(lead + (n, 2 * n))
    return t[..., n:]


def _nsa_bias_tables(rel_bias, n_cmp):
    n_max = Q_BLOCK * N_TOEP + BIAS_PAD
    dist = jnp.arange(-BIAS_PAD, n_max + 1)
    far = rel_bias.astype(F32)[REL_BUCKETS - 1]
    ftab = ((rel_bias.astype(F32)[_rel_bucket(dist)] - far[None, :]) * LOG2E).T

    def family(shift, n_d):
        rows = []
        for d in range(n_d):
            c = Q_BLOCK * d + Q_BLOCK + shift + BIAS_PAD
            rows.append(ftab[:, c - 2 * Q_BLOCK:c + 1][:, ::-1])
        return _toeplitz(jnp.stack(rows, axis=1))

    toep = family(0, N_TOEP)
    toep = toep.reshape(NSA_GROUPS, NSA_HPG, N_TOEP, Q_BLOCK, Q_BLOCK).transpose(0, 2, 1, 3, 4)
    toep = toep.reshape(NSA_GROUPS, N_TOEP, NSA_HPG * Q_BLOCK, Q_BLOCK)
    cm = family(-CMP_STRIDE, N_TOEP - 1)[..., ::CMP_STRIDE]
    strip = jnp.concatenate([cm[:, d] for d in range(N_TOEP - 2, -1, -1)], axis=-1)
    fill = jnp.zeros((NSA_HEADS, Q_BLOCK, n_cmp - strip.shape[-1]), F32)
    cmpb = jnp.concatenate([strip, fill], axis=-1).reshape(NSA_GROUPS, NSA_HPG * Q_BLOCK, n_cmp)
    return toep, cmpb


def _compress_kernel(x_ref, pos_ref, w1a_ref, w1b_ref, w2_ref, o_ref):
    x = x_ref[...].astype(F32)
    pos = pos_ref[...]
    a = _dot((x + pos[0:1, :]).astype(BF16), w1a_ref[...])
    b = _dot((x + pos[1:2, :]).astype(BF16), w1b_ref[...])
    n = x.shape[0]
    hid = a + pltpu.roll(b, n - 1, axis=0)
    o_ref[...] = _dot(_silu(hid).astype(BF16), w2_ref[...]).astype(o_ref.dtype)


def nsa_compress(kv_cmp, cmp_pos, w1k, w2k, w1v, w2v, B, T):
    G, dh = NSA_GROUPS, HEAD_DIM
    half = CMP_BLOCK // 2
    nch = T // half
    x = kv_cmp.reshape(B, nch, half, 2, G, dh).transpose(3, 0, 4, 1, 2, 5).reshape(2, B, G, nch, half * dh)
    pos = cmp_pos.astype(F32).reshape(2, 2, half * dh)
    w1 = jnp.stack([w1k, w1v]).astype(BF16)
    w2 = jnp.stack([w2k, w2v]).astype(BF16)
    return pl.pallas_call(
        _compress_kernel,
        grid=(2, B, G),
        in_specs=[pl.BlockSpec((None, None, None, nch, half * dh), lambda s, b, g: (s, b, g, 0, 0)),
                  pl.BlockSpec((None, 2, half * dh), lambda s, b, g: (s, 0, 0)),
                  pl.BlockSpec((None, half * dh, CMP_HIDDEN), lambda s, b, g: (s, 0, 0)),
                  pl.BlockSpec((None, half * dh, CMP_HIDDEN), lambda s, b, g: (s, 1, 0)),
                  pl.BlockSpec((None, CMP_HIDDEN, dh), lambda s, b, g: (s, 0, 0))],
        out_specs=pl.BlockSpec((None, None, None, nch, dh), lambda s, b, g: (s, b, g, 0, 0)),
        out_shape=jax.ShapeDtypeStruct((2, B, G, nch, dh), BF16),
        compiler_params=_cparams(("arbitrary", "arbitrary", "arbitrary")), name="nsa_compress",
    )(x, pos, w1, w1, w2)


SLC_TILE = 512
SM_ROWS = 64
WIN_KEYS = WINDOW + Q_BLOCK


def _nsa_attn_kernel(q_ref, kc_ref, vc_ref, ksa_ref, vs_ref, kw_ref, vw_ref, gate_ref, toep_ref, cmpb_ref,
                     ovl_ref, o_ref, qa_ref, s0_ref, s1_ref, p_ref, al_ref, oc_ref, ow_ref, acc_ref, m_ref, l_ref):
    qi = pl.program_id(2)
    qs = qi * Q_BLOCK
    HPG, dh, QB = NSA_HPG, HEAD_DIM, Q_BLOCK
    R = HPG * QB
    q = q_ref[...]
    Q = jnp.concatenate([q[:, h * dh:(h + 1) * dh] for h in range(HPG)], axis=0)
    tq = qs + lax.broadcasted_iota(jnp.int32, (QB, 1), 0)

    n_cmp = kc_ref.shape[0]
    s_all = _dot_nt(Q, kc_ref[...])
    shift = lax.rem(8 * (qi - 7) + n_cmp, n_cmp)
    cidx = lax.broadcasted_iota(jnp.int32, (1, n_cmp), 1)
    valid = (cidx * CMP_STRIDE + (CMP_BLOCK - 1)) <= tq
    p4 = None
    pn16 = []
    for h in range(HPG):
        rows = slice(h * QB, (h + 1) * QB)
        bias_h = pltpu.roll(cmpb_ref[rows, :], shift, axis=1)
        s = jnp.where(valid, s_all[rows] + bias_h, NEG_INF)
        m = jnp.max(s, axis=1, keepdims=True)
        p = jnp.where(valid, jnp.exp2(s - m), 0.0)
        l = jnp.sum(p, axis=1, keepdims=True)
        pn = p * jnp.where(l > 0.0, 1.0 / jnp.where(l > 0.0, l, 1.0), 0.0)
        p4 = pn if p4 is None else p4 + pn
        pn16.append(pn.astype(BF16))
    oc_ref[...] = _dot(jnp.concatenate(pn16, axis=0), vc_ref[...])

    imp = _dot_x3(p4, ovl_ref[...])
    n_slc = imp.shape[1]
    blk = lax.broadcasted_iota(jnp.int32, (QB, n_slc), 1)
    cur = _shr(tq, SLC_BLOCK)
    forced = (blk == 0) | (blk == cur) | (blk == cur - 1)
    work = jnp.where(forced, FORCE_SCORE, jnp.where(blk <= cur, imp, -1.0))
    work = work.T
    sidx = lax.broadcasted_iota(jnp.int32, (n_slc, QB), 0).astype(F32)
    sel = jnp.zeros((n_slc, QB), F32)
    for _ in range(min(SLC_TOP_N, n_slc)):
        mx = jnp.max(work, axis=0, keepdims=True)
        first = jnp.min(jnp.where(work == mx, sidx, float(n_slc)), axis=0, keepdims=True)
        hit = sidx == first
        sel = jnp.where(hit, 1.0, sel)
        work = jnp.where(hit, -2.0, work)
    selneg = jnp.where(sel > 0.5, 0.0, NEG_INF).T.astype(BF16)
    if n_slc < LANES:
        selneg = jnp.concatenate([selneg, jnp.zeros((QB, LANES - n_slc), BF16)], axis=1)
    qa_ref[:, 0:dh] = Q
    for h in range(HPG):
        qa_ref[h * QB:(h + 1) * QB, dh:] = selneg

    m_ref[...] = jnp.full(m_ref.shape, NEG_INF, F32)
    l_ref[...] = jnp.zeros(l_ref.shape, F32)
    acc_ref[...] = jnp.zeros(acc_ref.shape, F32)
    start = pl.multiple_of(jnp.maximum(qs - WINDOW, 0), QB)
    kwin = kw_ref[pl.ds(start, WIN_KEYS), :]
    vwin = vw_ref[pl.ds(start, WIN_KEYS), :]
    sw_all = _dot_nt(Q, kwin)
    doff = (qs - start) // QB
    n_wt = WIN_KEYS // QB
    wmask = []
    for t in range(n_wt):
        dist = tq - (start + t * QB + lax.broadcasted_iota(jnp.int32, (1, QB), 1))
        wmask.append(jnp.where((dist >= 0) & (dist < WINDOW), 0.0, NEG_INF))
    pw16 = []
    inv_lw = []
    for h in range(HPG):
        rows = slice(h * QB, (h + 1) * QB)
        parts = [sw_all[rows, t * QB:(t + 1) * QB] + toep_ref[jnp.clip(doff - t, 0, N_TOEP - 1), rows, :]
                 + wmask[t] for t in range(n_wt)]
        mw = parts[0]
        for t in range(1, n_wt):
            mw = jnp.maximum(mw, parts[t])
        mw = jnp.max(mw, axis=1, keepdims=True)
        pw = [jnp.exp2(pt - mw) for pt in parts]
        lw = pw[0]
        for t in range(1, n_wt):
            lw = lw + pw[t]
        inv_lw.append(1.0 / jnp.sum(lw, axis=1, keepdims=True))
        pw16.append(jnp.concatenate(pw, axis=1).astype(BF16))
    ow_ref[...] = _dot(jnp.concatenate(pw16, axis=0), vwin) * jnp.concatenate(inv_lw, axis=0)

    quarters = SLC_TILE // QB
    n_lane_tiles = SLC_TILE // LANES
    n_tiles = qi // quarters + 1
    last_tile = ksa_ref.shape[0] // SLC_TILE - 1
    j_near = jnp.maximum((qi - (N_TOEP - 2)) // quarters, 0)

    def scores(j, s_ref):
        k0 = pl.multiple_of(jnp.minimum(j, last_tile) * SLC_TILE, SLC_TILE)
        s_ref[...] = _dot_nt(qa_ref[...], ksa_ref[pl.ds(k0, SLC_TILE), :])

    def add_bias(j, s_ref):
        d0 = qi - quarters * j
        kpos = j * SLC_TILE + lax.broadcasted_iota(jnp.int32, (1, SLC_TILE), 1)
        causal = jnp.where(kpos <= tq, 0.0, NEG_INF)
        for h in range(HPG):
            rows = slice(h * QB, (h + 1) * QB)
            delta = jnp.concatenate(
                [toep_ref[jnp.clip(d0 - t, 0, N_TOEP - 1), rows, :] for t in range(quarters)], axis=1)
            s_ref[rows, :] = s_ref[rows, :] + delta + causal

    def softmax_pv(j, s_ref):
        for c in range(R // SM_ROWS):
            rows = slice(c * SM_ROWS, (c + 1) * SM_ROWS)
            parts = [s_ref[rows, t * LANES:(t + 1) * LANES] for t in range(n_lane_tiles)]
            mc = parts[0]
            for t in range(1, n_lane_tiles):
                mc = jnp.maximum(mc, parts[t])
            m_prev = m_ref[rows]
            m_new = jnp.maximum(m_prev, jnp.max(mc, axis=1, keepdims=True))
            alpha = jnp.exp2(m_prev - m_new)
            ps = [jnp.exp2(pt - m_new) for pt in parts]
            row_sum = ps[0]
            for t in range(1, n_lane_tiles):
                row_sum = row_sum + ps[t]
            l_ref[rows] = alpha * l_ref[rows] + jnp.sum(row_sum, axis=1, keepdims=True)
            m_ref[rows] = m_new
            al_ref[rows] = alpha
            for t in range(n_lane_tiles):
                p_ref[rows, t * LANES:(t + 1) * LANES] = ps[t].astype(BF16)
        k0 = pl.multiple_of(j * SLC_TILE, SLC_TILE)
        acc_ref[...] = al_ref[...] * acc_ref[...] + _dot(p_ref[...], vs_ref[pl.ds(k0, SLC_TILE), :])

    scores(0, s0_ref)

    def slc_pair(i, carry):
        ja = 2 * i
        jb = ja + 1

        @pl.when(ja >= j_near)
        def _():
            add_bias(ja, s0_ref)

        scores(jb, s1_ref)
        softmax_pv(ja, s0_ref)

        @pl.when((jb < n_tiles) & (jb >= j_near))
        def _():
            add_bias(jb, s1_ref)

        @pl.when(jb < n_tiles)
        def _():
            scores(jb + 1, s0_ref)
            softmax_pv(jb, s1_ref)

        return carry

    lax.fori_loop(0, (n_tiles + 1) // 2, slc_pair, 0)
    o_slc = acc_ref[...] / l_ref[...]

    gt = _sigmoid(gate_ref[...])
    o_cmp = oc_ref[...]
    o_win = ow_ref[...]
    outs = []
    for h in range(HPG):
        rows = slice(h * QB, (h + 1) * QB)
        outs.append(gt[:, 3 * h:3 * h + 1] * o_cmp[rows] + gt[:, 3 * h + 1:3 * h + 2] * o_slc[rows]
                    + gt[:, 3 * h + 2:3 * h + 3] * o_win[rows])
    o_ref[...] = jnp.concatenate(outs, axis=1).astype(o_ref.dtype)


def nsa_attention(q, kv, kvc, gates, toep, cmpb, B, T):
    N = q.shape[0]
    G, HPG, dh, QB = NSA_GROUPS, NSA_HPG, HEAD_DIM, Q_BLOCK
    nq = T // QB
    n_cmp = T // CMP_STRIDE
    n_slc = T // SLC_BLOCK
    c = jnp.arange(n_cmp)[:, None] * CMP_STRIDE
    sblk = jnp.arange(n_slc)[None, :] * SLC_BLOCK
    ovl = ((c < sblk + SLC_BLOCK) & (c + CMP_BLOCK > sblk)).astype(BF16)

    assert n_slc <= LANES
    member = (jnp.arange(T)[:, None] // SLC_BLOCK == jnp.arange(LANES)[None, :]).astype(BF16)
    k_slc = kv[:, :, 2 * G * dh:3 * G * dh].reshape(B, T, G, dh).transpose(0, 2, 1, 3)
    ksa = jnp.concatenate([k_slc, jnp.broadcast_to(member, (B, G, T, LANES))], axis=-1)

    def kv_spec(branch, which):
        col = (branch * 2 + which) * G
        return pl.BlockSpec((None, T, dh), lambda b, g, qi: (b, 0, col + g))

    def cmp_spec(which):
        return pl.BlockSpec((None, None, None, n_cmp, dh), lambda b, g, qi: (which, b, g, 0, 0))

    R = HPG * QB
    return pl.pallas_call(
        _nsa_attn_kernel,
        grid=(B, G, nq),
        in_specs=[pl.BlockSpec((QB, HPG * dh), lambda b, g, qi: (b * nq + qi, g)),
                  cmp_spec(0), cmp_spec(1),
                  pl.BlockSpec((None, None, T, dh + LANES), lambda b, g, qi: (b, g, 0, 0)),
                  kv_spec(1, 1), kv_spec(2, 0), kv_spec(2, 1),
                  pl.BlockSpec((QB, LANES), lambda b, g, qi: (b * nq + qi, g)),
                  pl.BlockSpec((None, N_TOEP, R, QB), lambda b, g, qi: (g, 0, 0, 0)),
                  pl.BlockSpec((None, R, n_cmp), lambda b, g, qi: (g, 0, 0)),
                  pl.BlockSpec((n_cmp, n_slc), lambda b, g, qi: (0, 0))],
        out_specs=pl.BlockSpec((QB, HPG * dh), lambda b, g, qi: (b * nq + qi, g)),
        out_shape=jax.ShapeDtypeStruct((N, G * HPG * dh), BF16),
        scratch_shapes=[pltpu.VMEM((R, dh + LANES), BF16),
                        pltpu.VMEM((R, SLC_TILE), F32),
                        pltpu.VMEM((R, SLC_TILE), F32),
                        pltpu.VMEM((R, SLC_TILE), BF16),
                        pltpu.VMEM((R, LANES), F32),
                        pltpu.VMEM((R, dh), F32),
                        pltpu.VMEM((R, dh), F32),
                        pltpu.VMEM((R, dh), F32), pltpu.VMEM((R, LANES), F32), pltpu.VMEM((R, LANES), F32)],
        compiler_params=_cparams(("arbitrary", "arbitrary", "arbitrary"), vmem_mib=56), name="nsa_attention",
    )(q, kvc, kvc, ksa, kv, kv, kv, gates, toep, cmpb, ovl)


def nsa_mixer(h, w_in, cmp_pos, w1k, w2k, w1v, w2v, w_o, toep, cmpb, B, T):
    N, D = h.shape
    G, HPG, dh = NSA_GROUPS, NSA_HPG, HEAD_DIM
    q_dim = NSA_HEADS * dh
    kv_dim = 3 * 2 * G * dh
    wq = w_in[:, :q_dim].astype(BF16)
    wkv = w_in[:, q_dim:q_dim + kv_dim].astype(BF16)
    wg = w_in[:, q_dim + kv_dim:].astype(BF16).reshape(D, G, HPG * 3)
    wg = jnp.zeros((D, G, LANES), BF16).at[:, :, :HPG * 3].set(wg).reshape(D, G * LANES)
    q = matmul(h, wq, BF16, scale=dh ** -0.5 * LOG2E, name="nsa_in_q")
    kv = matmul(h, wkv, BF16, name="nsa_in_kv")
    gates = matmul(h, wg, F32, name="nsa_in_gate")
    kv3 = kv.reshape(B, T, kv_dim)
    kvc = nsa_compress(kv3[:, :, :2 * G * dh], cmp_pos, w1k, w2k, w1v, w2v, B, T)
    o = nsa_attention(q, kv3, kvc, gates, toep, cmpb, B, T)
    return matmul(o, w_o.astype(BF16), F32, name="nsa_out")


def kernel(x, c, rel_bias, ada_w, ada_b, ln_g, ln_b, nsa_w_in, nsa_cmp_pos, nsa_cmp_w1k, nsa_cmp_w2k, nsa_cmp_w1v, nsa_cmp_w2v, nsa_w_o, conv_w_in, conv_dw, conv_ln_g, conv_ln_b, conv_w_out, gdn_w_in, gdn_conv, gdn_a_log, gdn_dt_bias, gdn_norm_w, gdn_w_out, ffn_w1, ffn_w3, ffn_w2, moe_router, moe_w1, moe_w3, moe_w2):
    B, T, D = x.shape
    N = B * T
    depth = ada_w.shape[0]
    mod = adaln_all(c, ada_w, ada_b).reshape(depth, B, 6, 1, D)
    toep, cmpb = _nsa_bias_tables(rel_bias, T // CMP_STRIDE)
    x2 = x.reshape(N, D)

    def h_dtype_for(layer, sub):
        return F32 if (sub == 1 and layer % 2 == 1) else BF16

    h = modulate(x2, mod[0, :, 1], mod[0, :, 0], T, h_dtype_for(0, 0))
    for i in range(depth):
        sh1, sc1, g1, sh2, sc2, g2 = (mod[i, :, k] for k in range(6))
        kind, j = i % 3, i // 3
        if kind == 0:
            y = nsa_mixer(h, nsa_w_in[j], nsa_cmp_pos[j], nsa_cmp_w1k[j], nsa_cmp_w2k[j], nsa_cmp_w1v[j],
                          nsa_cmp_w2v[j], nsa_w_o[j], toep, cmpb, B, T)
        elif kind == 1:
            y = conformer_mixer(h, conv_w_in[j], conv_dw[j], conv_ln_g[j], conv_ln_b[j], conv_w_out[j], T)
        else:
            y = gdn_mixer(h, gdn_w_in[j], gdn_conv[j], gdn_a_log[j], gdn_dt_bias[j], gdn_norm_w[j],
                          gdn_w_out[j], B, T)
        lg = ln_g[i].reshape(2, 1, D)
        lb = ln_b[i].reshape(2, 1, D)
        x2, h = norm_mod(x2, y, g1, lg[0], lb[0], sc2, sh2, T, h_dtype_for(i, 1))
        m = i // 2
        last = i == depth - 1
        sc_n = None if last else mod[i + 1, :, 1]
        sh_n = None if last else mod[i + 1, :, 0]
        if i % 2 == 0:
            y = ffn_swiglu(h, ffn_w1, ffn_w3, ffn_w2, m)
            x2, h = norm_mod(x2, y, g2, lg[1], lb[1], sc_n, sh_n, T, h_dtype_for(i + 1, 0))
        else:
            x2, h = moe_sublayer(h, x2, moe_router[m], moe_w1, moe_w3, moe_w2, m,
                                 g2, lg[1], lb[1], sc_n, sh_n, T, h_dtype_for(i + 1, 0))
    return x2.reshape(B, T, D)
```

```python
import functools
import math

import jax
import jax.numpy as jnp
from jax import lax
from jax.experimental import pallas as pl
from jax.experimental.pallas import tpu as pltpu

F32 = jnp.float32
BF16 = jnp.bfloat16

DEPTH = 4
ALPHA = (2 * DEPTH) ** 0.25
LN_EPS = 1e-5
NEG_INF = -1e30
LOG2E = math.log2(math.e)

REL_BUCKETS = 32
REL_EXACT = 16
REL_MAX_DIST = 1024

NSA_HEADS = 16
NSA_GROUPS = 4
NSA_HPG = NSA_HEADS // NSA_GROUPS
HEAD_DIM = 128
CMP_BLOCK = 32
CMP_STRIDE = 16
CMP_HIDDEN = 256
SLC_BLOCK = 64
SLC_TOP_N = 16
WINDOW = 512
Q_BLOCK = 128
FORCE_SCORE = 1e6

CONV_WIDTH = 31
GDN_QK_HEADS = 16
GDN_V_HEADS = 32
GDN_CONV = 4
GDN_CHUNK = 64

N_EXPERTS = 8
TOP_K = 2

LANES = 128
SUBLANES = 8
MIB = 1 << 20


def _cparams(sem, vmem_mib=48):
    return pltpu.CompilerParams(dimension_semantics=sem, vmem_limit_bytes=vmem_mib * MIB)


def _dot(a, b):
    return jnp.dot(a, b, preferred_element_type=F32)


def _dot_nt(a, b):
    return lax.dot_general(a, b, (((1,), (1,)), ((), ())), preferred_element_type=F32)


def _split3(a):
    hi = a.astype(BF16)
    r1 = a - hi.astype(F32)
    mid = r1.astype(BF16)
    lo = (r1 - mid.astype(F32)).astype(BF16)
    return hi, mid, lo


def _dot_x3(a, b_bf16):
    hi, mid, lo = _split3(a)
    return _dot(hi, b_bf16) + _dot(mid, b_bf16) + _dot(lo, b_bf16)


def _dot_x2(a, b_bf16):
    hi = a.astype(BF16)
    lo = (a - hi.astype(F32)).astype(BF16)
    return _dot(hi, b_bf16) + _dot(lo, b_bf16)


def _silu(x):
    return x * (1.0 / (1.0 + jnp.exp(-x)))


def _sigmoid(x):
    return 1.0 / (1.0 + jnp.exp(-x))


def _softplus(x):
    return jnp.maximum(x, 0.0) + jnp.log(1.0 + jnp.exp(-jnp.abs(x)))


def _shr(x, pow2):
    return lax.shift_right_logical(x, jnp.full(x.shape, int(math.log2(pow2)), jnp.int32))


def _adaln_kernel(ct_ref, w_ref, b_ref, o_ref):
    w = w_ref[...]
    tn = w.shape[1]
    rows = []
    for b in range(ct_ref.shape[0]):
        cb = _silu(ct_ref[b])
        parts = [jnp.sum(w[:, j * LANES:(j + 1) * LANES] * cb, axis=0, keepdims=True)
                 for j in range(tn // LANES)]
        rows.append(jnp.concatenate(parts, axis=1))
    o_ref[...] = jnp.concatenate(rows, axis=0) + b_ref[...]


def adaln_all(c, ada_w, ada_b):
    B, D = c.shape
    depth, _, n_out = ada_w.shape
    tn = 1024
    ct = jnp.broadcast_to(c[:, :, None], (B, D, LANES))
    return pl.pallas_call(
        _adaln_kernel,
        grid=(depth, n_out // tn),
        in_specs=[pl.BlockSpec((B, D, LANES), lambda i, j: (0, 0, 0)),
                  pl.BlockSpec((None, D, tn), lambda i, j: (i, 0, j)),
                  pl.BlockSpec((None, 1, tn), lambda i, j: (i, 0, j))],
        out_specs=pl.BlockSpec((None, B, tn), lambda i, j: (i, 0, j)),
        out_shape=jax.ShapeDtypeStruct((depth, B, n_out), F32),
        compiler_params=_cparams(("arbitrary", "arbitrary")),
        name="adaln",
    )(ct, ada_w, ada_b.reshape(depth, 1, n_out))


def _mod_kernel(x_ref, sc_ref, sh_ref, h_ref):
    h_ref[...] = (x_ref[...] * (1.0 + sc_ref[...]) + sh_ref[...]).astype(h_ref.dtype)


def modulate(x2, sc, sh, T, h_dtype):
    N, D = x2.shape
    tm = 512
    vec = pl.BlockSpec((None, 1, D), lambda i: ((i * tm) // T, 0, 0))
    return pl.pallas_call(
        _mod_kernel,
        grid=(N // tm,),
        in_specs=[pl.BlockSpec((tm, D), lambda i: (i, 0)), vec, vec],
        out_specs=pl.BlockSpec((tm, D), lambda i: (i, 0)),
        out_shape=jax.ShapeDtypeStruct((N, D), h_dtype),
        compiler_params=_cparams(("arbitrary",)),
        name="modulate",
    )(x2, sc, sh)


def _deepnorm(x, y, gate, lg, lb):
    z = ALPHA * x + (1.0 + gate) * y
    mu = jnp.mean(z, axis=-1, keepdims=True)
    zc = z - mu
    var = jnp.mean(zc * zc, axis=-1, keepdims=True)
    return zc * lax.rsqrt(var + LN_EPS) * lg + lb


def _norm_mod_kernel(x_ref, y_ref, g_ref, lg_ref, lb_ref, sc_ref, sh_ref, xo_ref, ho_ref):
    xn = _deepnorm(x_ref[...], y_ref[...].astype(F32), g_ref[...], lg_ref[...], lb_ref[...])
    xo_ref[...] = xn
    ho_ref[...] = (xn * (1.0 + sc_ref[...]) + sh_ref[...]).astype(ho_ref.dtype)


def _norm_kernel(x_ref, y_ref, g_ref, lg_ref, lb_ref, xo_ref):
    xo_ref[...] = _deepnorm(x_ref[...], y_ref[...].astype(F32), g_ref[...], lg_ref[...], lb_ref[...])


def norm_mod(x2, y, gate, lg, lb, sc, sh, T, h_dtype):
    N, D = x2.shape
    tm = 256
    row = pl.BlockSpec((tm, D), lambda i: (i, 0))
    vec = pl.BlockSpec((None, 1, D), lambda i: ((i * tm) // T, 0, 0))
    par = pl.BlockSpec((1, D), lambda i: (0, 0))
    if sc is None:
        return pl.pallas_call(
            _norm_kernel, grid=(N // tm,),
            in_specs=[row, row, vec, par, par], out_specs=row,
            out_shape=jax.ShapeDtypeStruct((N, D), F32),
            compiler_params=_cparams(("arbitrary",)), name="deepnorm",
        )(x2, y, gate, lg, lb), None
    return pl.pallas_call(
        _norm_mod_kernel, grid=(N // tm,),
        in_specs=[row, row, vec, par, par, vec, vec], out_specs=[row, row],
        out_shape=[jax.ShapeDtypeStruct((N, D), F32), jax.ShapeDtypeStruct((N, D), h_dtype)],
        compiler_params=_cparams(("arbitrary",)), name="deepnorm_mod",
    )(x2, y, gate, lg, lb, sc, sh)


def _mm_kernel(x_ref, w_ref, o_ref, *, scale):
    acc = _dot(x_ref[...], w_ref[...])
    if scale != 1.0:
        acc = acc * scale
    o_ref[...] = acc.astype(o_ref.dtype)


def matmul(x, w, out_dtype, *, tm=1024, tn=512, scale=1.0, name="matmul"):
    M, K = x.shape
    N = w.shape[1]
    tn = min(tn, N)
    return pl.pallas_call(
        functools.partial(_mm_kernel, scale=scale),
        grid=(M // tm, N // tn),
        in_specs=[pl.BlockSpec((tm, K), lambda i, j: (i, 0)),
                  pl.BlockSpec((K, tn), lambda i, j: (0, j))],
        out_specs=pl.BlockSpec((tm, tn), lambda i, j: (i, j)),
        out_shape=jax.ShapeDtypeStruct((M, N), out_dtype),
        compiler_params=_cparams(("arbitrary", "arbitrary")),
        name=name,
    )(x, w)


def _mm_glu_kernel(x_ref, wa_ref, wb_ref, o_ref):
    x = x_ref[...]
    a = _dot(x, wa_ref[...])
    b = _dot(x, wb_ref[...])
    o_ref[...] = (a * _sigmoid(b)).astype(o_ref.dtype)


def matmul_glu(x, w, out_dtype, *, tm=1024, tn=512):
    M, K = x.shape
    n = w.shape[1] // 2
    nj = n // tn
    return pl.pallas_call(
        _mm_glu_kernel,
        grid=(M // tm, nj),
        in_specs=[pl.BlockSpec((tm, K), lambda i, j: (i, 0)),
                  pl.BlockSpec((K, tn), lambda i, j: (0, j)),
                  pl.BlockSpec((K, tn), lambda i, j: (0, j + nj))],
        out_specs=pl.BlockSpec((tm, tn), lambda i, j: (i, j)),
        out_shape=jax.ShapeDtypeStruct((M, n), out_dtype),
        compiler_params=_cparams(("arbitrary", "arbitrary")),
        name="matmul_glu",
    )(x, w, w)


def _ffn_kernel(x_ref, w1_ref, w3_ref, w2_ref, o_ref):
    f = pl.program_id(1)
    @pl.when(f == 0)
    def _():
        o_ref[...] = jnp.zeros_like(o_ref)

    x = x_ref[...]
    a = _dot(x, w1_ref[...].astype(BF16))
    b = _dot(x, w3_ref[...].astype(BF16))
    o_ref[...] += _dot((_silu(a) * b).astype(BF16), w2_ref[...].astype(BF16))


def ffn_swiglu(h, w1, w3, w2, layer, *, tm=1024, tf=256):
    N, D = h.shape
    Fd = w1.shape[2]
    single = pl.Buffered(1)
    return pl.pallas_call(
        _ffn_kernel,
        grid=(N // tm, Fd // tf),
        in_specs=[pl.BlockSpec((tm, D), lambda i, f: (i, 0), pipeline_mode=single),
                  pl.BlockSpec((None, D, tf), lambda i, f: (layer, 0, f)),
                  pl.BlockSpec((None, D, tf), lambda i, f: (layer, 0, f)),
                  pl.BlockSpec((None, tf, D), lambda i, f: (layer, f, 0))],
        out_specs=pl.BlockSpec((tm, D), lambda i, f: (i, 0), pipeline_mode=single),
        out_shape=jax.ShapeDtypeStruct((N, D), F32),
        compiler_params=_cparams(("arbitrary", "arbitrary")),
        name="ffn_swiglu",
    )(h, w1, w3, w2)


def _router_kernel(h_ref, rh_ref, rl_ref, idx_ref, wt_ref):
    h = h_ref[...]
    hh = h.astype(BF16)
    hl = (h - hh.astype(F32)).astype(BF16)
    rh = rh_ref[...]
    logits = _dot(hh, rh) + _dot(hl, rh) + _dot(hh, rl_ref[...])
    lane = lax.broadcasted_iota(jnp.int32, logits.shape, 1)
    lanef = lane.astype(F32)
    logits = jnp.where(lane < N_EXPERTS, logits, -jnp.inf)
    m1 = jnp.max(logits, axis=1, keepdims=True)
    i1 = jnp.min(jnp.where(logits == m1, lanef, float(LANES)), axis=1, keepdims=True)
    rest = jnp.where(lanef == i1, -jnp.inf, logits)
    m2 = jnp.max(rest, axis=1, keepdims=True)
    i2 = jnp.min(jnp.where(rest == m2, lanef, float(LANES)), axis=1, keepdims=True)
    e2 = jnp.exp(m2 - m1)
    w1 = 1.0 / (1.0 + e2)
    w2 = e2 / (1.0 + e2)
    idx_ref[...] = jnp.where(lane == 0, i1, jnp.where(lane == 1, i2, 0.0)).astype(jnp.int32)
    wt_ref[...] = jnp.where(lane == 0, w1, jnp.where(lane == 1, w2, 0.0))


def moe_router(h, router):
    N, D = h.shape
    tm = 512
    rp = jnp.zeros((D, LANES), F32).at[:, :N_EXPERTS].set(router)
    rh = rp.astype(BF16)
    rl = (rp - rh.astype(F32)).astype(BF16)
    row = pl.BlockSpec((tm, LANES), lambda i: (i, 0))
    return pl.pallas_call(
        _router_kernel, grid=(N // tm,),
        in_specs=[pl.BlockSpec((tm, D), lambda i: (i, 0)),
                  pl.BlockSpec((D, LANES), lambda i: (0, 0)),
                  pl.BlockSpec((D, LANES), lambda i: (0, 0))],
        out_specs=[row, row],
        out_shape=[jax.ShapeDtypeStruct((N, LANES), jnp.int32), jax.ShapeDtypeStruct((N, LANES), F32)],
        compiler_params=_cparams(("arbitrary",)), name="moe_router",
    )(h, rh, rl)


MOE_ROW_GROUPS = 4
DMA_UNROLL = 8


def _row_copy(src_hbm, dst_ref, sem, src_row, dst_row):
    return pltpu.make_async_copy(src_hbm.at[pl.ds(src_row, 1), :], dst_ref.at[pl.ds(dst_row, 1), :], sem)


def _row_put(src_ref, dst_hbm, sem, src_row, dst_row):
    return pltpu.make_async_copy(src_ref.at[pl.ds(src_row, 1), :], dst_hbm.at[pl.ds(dst_row, 1), :], sem)


def _dispatch_rows_kernel(pos_ref, x_ref, init_hbm, o_hbm, sem, *, rows):
    del init_hbm
    base = pl.program_id(0) * rows

    def start(r, carry):
        for k in range(TOP_K):
            _row_put(x_ref, o_hbm, sem, r, pos_ref[TOP_K * (base + r) + k]).start()
        return carry

    def wait(r, carry):
        for k in range(TOP_K):
            _row_put(x_ref, o_hbm, sem, r, pos_ref[TOP_K * (base + r) + k]).wait()
        return carry

    lax.fori_loop(0, rows, start, 0, unroll=DMA_UNROLL)
    lax.fori_loop(0, rows, wait, 0, unroll=DMA_UNROLL)


def dispatch_rows(x, pos, n_out, *, rows=256):
    N, D = x.shape
    return pl.pallas_call(
        functools.partial(_dispatch_rows_kernel, rows=rows),
        grid_spec=pltpu.PrefetchScalarGridSpec(
            num_scalar_prefetch=1, grid=(N // rows,),
            in_specs=[pl.BlockSpec((rows, D), lambda i, p: (i, 0)), pl.BlockSpec(memory_space=pl.ANY)],
            out_specs=pl.BlockSpec(memory_space=pl.ANY),
            scratch_shapes=[pltpu.SemaphoreType.DMA(())]),
        out_shape=jax.ShapeDtypeStruct((n_out, D), x.dtype),
        input_output_aliases={2: 0},
        compiler_params=_cparams(("arbitrary",)), name="moe_dispatch",
    )(pos, x, jnp.zeros((n_out, D), x.dtype))


def _moe_mm_kernel(te_ref, tv_ref, x_ref, w1_ref, w3_ref, w2_ref, o_ref, xb_ref):
    i = pl.program_id(0)
    f = pl.program_id(1)

    @pl.when(f == 0)
    def _():
        xb_ref[...] = x_ref[...].astype(BF16)
        o_ref[...] = jnp.zeros_like(o_ref)

    group = xb_ref.shape[0] // MOE_ROW_GROUPS
    for n_groups in range(1, MOE_ROW_GROUPS + 1):
        @pl.when(tv_ref[i] == n_groups)
        def _(rows=n_groups * group):
            x = xb_ref[0:rows, :]
            a = _dot(x, w1_ref[...].astype(BF16))
            b = _dot(x, w3_ref[...].astype(BF16))
            o_ref[0:rows, :] += _dot((_silu(a) * b).astype(BF16), w2_ref[...].astype(BF16))


def moe_grouped_swiglu(xs, tile_expert, tile_valid, w1, w3, w2, layer, *, tm, tf=256):
    R, D = xs.shape
    Fd = w1.shape[3]
    single = pl.Buffered(1)
    return pl.pallas_call(
        _moe_mm_kernel,
        grid_spec=pltpu.PrefetchScalarGridSpec(
            num_scalar_prefetch=2, grid=(R // tm, Fd // tf),
            in_specs=[pl.BlockSpec((tm, D), lambda i, f, te, tv: (i, 0), pipeline_mode=single),
                      pl.BlockSpec((None, None, D, tf),
                                   lambda i, f, te, tv: (layer, te[i], 0, f * jnp.minimum(tv[i], 1))),
                      pl.BlockSpec((None, None, D, tf),
                                   lambda i, f, te, tv: (layer, te[i], 0, f * jnp.minimum(tv[i], 1))),
                      pl.BlockSpec((None, None, tf, D),
                                   lambda i, f, te, tv: (layer, te[i], f * jnp.minimum(tv[i], 1), 0))],
            out_specs=pl.BlockSpec((tm, D), lambda i, f, te, tv: (i, 0), pipeline_mode=single),
            scratch_shapes=[pltpu.VMEM((tm, D), BF16)]),
        out_shape=jax.ShapeDtypeStruct((R, D), F32),
        compiler_params=_cparams(("arbitrary", "arbitrary"), vmem_mib=52), name="moe_grouped_swiglu",
    )(tile_expert, tile_valid, xs, w1, w3, w2)


def _combine_gather(pos_ref, ys_hbm, buf_ref, sem, base, rows):
    def start(r, carry):
        _row_copy(ys_hbm, buf_ref.at[0], sem, pos_ref[2 * (base + r)], r).start()
        _row_copy(ys_hbm, buf_ref.at[1], sem, pos_ref[2 * (base + r) + 1], r).start()
        return carry

    def wait(r, carry):
        _row_copy(ys_hbm, buf_ref.at[0], sem, pos_ref[2 * (base + r)], r).wait()
        _row_copy(ys_hbm, buf_ref.at[1], sem, pos_ref[2 * (base + r) + 1], r).wait()
        return carry

    lax.fori_loop(0, rows, start, 0, unroll=DMA_UNROLL)
    lax.fori_loop(0, rows, wait, 0, unroll=DMA_UNROLL)


def _moe_combine(wt_ref, buf_ref):
    wt = wt_ref[...]
    return wt[:, 0:1] * buf_ref[0] + wt[:, 1:2] * buf_ref[1]


def _combine_norm_mod_kernel(pos_ref, ys_hbm, wt_ref, x_ref, g_ref, lg_ref, lb_ref, sc_ref, sh_ref,
                             xo_ref, ho_ref, buf_ref, sem, *, rows):
    _combine_gather(pos_ref, ys_hbm, buf_ref, sem, pl.program_id(0) * rows, rows)
    xn = _deepnorm(x_ref[...], _moe_combine(wt_ref, buf_ref), g_ref[...], lg_ref[...], lb_ref[...])
    xo_ref[...] = xn
    ho_ref[...] = (xn * (1.0 + sc_ref[...]) + sh_ref[...]).astype(ho_ref.dtype)


def _combine_norm_kernel(pos_ref, ys_hbm, wt_ref, x_ref, g_ref, lg_ref, lb_ref, xo_ref, buf_ref, sem, *, rows):
    _combine_gather(pos_ref, ys_hbm, buf_ref, sem, pl.program_id(0) * rows, rows)
    xo_ref[...] = _deepnorm(x_ref[...], _moe_combine(wt_ref, buf_ref), g_ref[...], lg_ref[...], lb_ref[...])


def moe_combine_norm_mod(ys, pos, wts, x2, gate, lg, lb, sc, sh, T, h_dtype, *, rows=256):
    N, D = x2.shape
    row = pl.BlockSpec((rows, D), lambda i, p: (i, 0))
    vec = pl.BlockSpec((None, 1, D), lambda i, p: ((i * rows) // T, 0, 0))
    par = pl.BlockSpec((1, D), lambda i, p: (0, 0))
    wsp = pl.BlockSpec((rows, LANES), lambda i, p: (i, 0))
    anysp = pl.BlockSpec(memory_space=pl.ANY)
    scratch = [pltpu.VMEM((2, rows, D), F32), pltpu.SemaphoreType.DMA(())]
    if sc is None:
        out = pl.pallas_call(
            functools.partial(_combine_norm_kernel, rows=rows),
            grid_spec=pltpu.PrefetchScalarGridSpec(
                num_scalar_prefetch=1, grid=(N // rows,),
                in_specs=[anysp, wsp, row, vec, par, par], out_specs=row, scratch_shapes=scratch),
            out_shape=jax.ShapeDtypeStruct((N, D), F32),
            compiler_params=_cparams(("arbitrary",)), name="moe_combine_norm",
        )(pos, ys, wts, x2, gate, lg, lb)
        return out, None
    return pl.pallas_call(
        functools.partial(_combine_norm_mod_kernel, rows=rows),
        grid_spec=pltpu.PrefetchScalarGridSpec(
            num_scalar_prefetch=1, grid=(N // rows,),
            in_specs=[anysp, wsp, row, vec, par, par, vec, vec], out_specs=[row, row],
            scratch_shapes=scratch),
        out_shape=[jax.ShapeDtypeStruct((N, D), F32), jax.ShapeDtypeStruct((N, D), h_dtype)],
        compiler_params=_cparams(("arbitrary",)), name="moe_combine_norm_mod",
    )(pos, ys, wts, x2, gate, lg, lb, sc, sh)


def _route_tables(idx2, tm, n_tiles):
    e_flat = idx2.reshape(-1)
    onehot = (jnp.arange(N_EXPERTS, dtype=jnp.int32)[:, None] == e_flat[None, :]).astype(jnp.int32)
    csum = jnp.cumsum(onehot, axis=1)
    counts = csum[:, -1]
    rank = jnp.sum(onehot * (csum - 1), axis=0)
    tiles_per = (counts + tm - 1) // tm
    tile_end = jnp.cumsum(tiles_per)
    pad_start = (tile_end - tiles_per) * tm
    t = jnp.arange(n_tiles)
    te = jnp.sum((t[:, None] >= tile_end[None, :]).astype(jnp.int32), axis=1)
    tile_expert = jnp.minimum(te, N_EXPERTS - 1).astype(jnp.int32)
    first_tile = (tile_end - tiles_per)[tile_expert]
    used_rows = jnp.where(te < N_EXPERTS, jnp.clip(counts[tile_expert] - (t - first_tile) * tm, 0, tm), 0)
    group = tm // MOE_ROW_GROUPS
    tile_valid = ((used_rows + group - 1) // group).astype(jnp.int32)
    pos = (jnp.sum(onehot * pad_start[:, None], axis=0) + rank).astype(jnp.int32)
    return tile_expert, tile_valid, pos


def moe_sublayer(h, x2, router, w1, w3, w2, layer, gate, lg, lb, sc, sh, T, h_dtype, *, tm=1024):
    N, D = h.shape
    idx, wts = moe_router(h, router)
    n_tiles = (N * TOP_K) // tm + N_EXPERTS
    tile_expert, tile_valid, pos = _route_tables(idx[:, :TOP_K], tm, n_tiles)
    xs = dispatch_rows(h, pos, n_tiles * tm)
    ys = moe_grouped_swiglu(xs, tile_expert, tile_valid, w1, w3, w2, layer, tm=tm)
    return moe_combine_norm_mod(ys, pos, wts, x2, gate, lg, lb, sc, sh, T, h_dtype)


DW_HALO = 32
DW_ROWS = 128
DW_PITCH = 3
DW_GROUP = 4


def _dwconv_kernel(u_ref, halo_ref, dw_ref, lg_ref, lb_ref, o_ref, ext_ref, acc_ref, *, tiles_per_seq):
    i = pl.program_id(0)
    first = (i % tiles_per_seq) == 0
    lead = DW_HALO - (CONV_WIDTH - 1)
    P = DW_PITCH
    lg = lg_ref[...]
    lb = lb_ref[...]
    for sl in range(u_ref.shape[1] // LANES):
        cols = slice(sl * LANES, (sl + 1) * LANES)
        ext_ref[sl, pl.ds(0, DW_HALO, stride=P), :] = jnp.where(first, 0.0, halo_ref[:, cols].astype(F32))
        ext_ref[sl, pl.ds(P * DW_HALO, DW_ROWS, stride=P), :] = u_ref[:, cols].astype(F32)
    for sl in range(u_ref.shape[1] // LANES):
        cols = slice(sl * LANES, (sl + 1) * LANES)
        for rg in range(DW_ROWS // (SUBLANES * DW_GROUP)):
            accs = [None] * DW_GROUP
            for k in range(CONV_WIDTH):
                w = dw_ref[k:k + 1, cols]
                for t in range(DW_GROUP):
                    r0 = (rg * DW_GROUP + t) * SUBLANES
                    term = ext_ref[sl, pl.ds(P * (r0 + lead + k), SUBLANES, stride=P), :] * w
                    accs[t] = term if accs[t] is None else accs[t] + term
            for t in range(DW_GROUP):
                r0 = (rg * DW_GROUP + t) * SUBLANES
                acc_ref[r0:r0 + SUBLANES, cols] = accs[t]
    pack_rows = 2 * SUBLANES
    for rc in range(DW_ROWS // pack_rows):
        r0 = rc * pack_rows
        acc = acc_ref[r0:r0 + pack_rows, :]
        mu = jnp.mean(acc, axis=-1, keepdims=True)
        zc = acc - mu
        var = jnp.mean(zc * zc, axis=-1, keepdims=True)
        v = zc * lax.rsqrt(var + LN_EPS) * lg + lb
        o_ref[r0:r0 + pack_rows, :] = _silu(v).astype(o_ref.dtype)


def dwconv_ln_silu(u, dw, lg, lb, T):
    N, D = u.shape
    hb = DW_ROWS // DW_HALO
    return pl.pallas_call(
        functools.partial(_dwconv_kernel, tiles_per_seq=T // DW_ROWS),
        grid=(N // DW_ROWS,),
        in_specs=[pl.BlockSpec((DW_ROWS, D), lambda i: (i, 0)),
                  pl.BlockSpec((DW_HALO, D), lambda i: (jnp.maximum(i * hb - 1, 0), 0)),
                  pl.BlockSpec((CONV_WIDTH, D), lambda i: (0, 0)),
                  pl.BlockSpec((1, D), lambda i: (0, 0)),
                  pl.BlockSpec((1, D), lambda i: (0, 0))],
        out_specs=pl.BlockSpec((DW_ROWS, D), lambda i: (i, 0)),
        out_shape=jax.ShapeDtypeStruct((N, D), BF16),
        scratch_shapes=[pltpu.VMEM((D // LANES, DW_PITCH * (DW_HALO + DW_ROWS), LANES), F32),
                        pltpu.VMEM((DW_ROWS, D), F32)],
        compiler_params=_cparams(("arbitrary",)), name="dwconv_ln_silu",
    )(u, u, dw, lg.reshape(1, D), lb.reshape(1, D))


def conformer_mixer(h, w_in, dw, lg, lb, w_out, T):
    u = matmul_glu(h, w_in.astype(BF16), BF16)
    v = dwconv_ln_silu(u, dw, lg, lb, T)
    return matmul(v, w_out.astype(BF16), F32, name="conv_out")


GC_HALO = 16
GC_ROWS = 512
GC_COLS = 1024


def _gconv_kernel(x_ref, halo_ref, w_ref, o_ref, ext_ref, *, tiles_per_seq):
    i = pl.program_id(0)
    first = (i % tiles_per_seq) == 0
    lead = GC_HALO - (GDN_CONV - 1)
    P = DW_PITCH
    pack_rows = 2 * SUBLANES
    for sl in range(GC_COLS // LANES):
        cols = slice(sl * LANES, (sl + 1) * LANES)
        ext_ref[sl, pl.ds(0, GC_HALO, stride=P), :] = jnp.where(first, 0.0, halo_ref[:, cols].astype(F32))
        ext_ref[sl, pl.ds(P * GC_HALO, GC_ROWS, stride=P), :] = x_ref[:, cols].astype(F32)
    for sl in range(GC_COLS // LANES):
        cols = slice(sl * LANES, (sl + 1) * LANES)
        taps = [w_ref[k:k + 1, cols] for k in range(GDN_CONV)]
        for rc in range(GC_ROWS // pack_rows):
            halves = []
            for r0 in (rc * pack_rows, rc * pack_rows + SUBLANES):
                acc = None
                for k in range(GDN_CONV):
                    term = ext_ref[sl, pl.ds(P * (r0 + lead + k), SUBLANES, stride=P), :] * taps[k]
                    acc = term if acc is None else acc + term
                halves.append(acc)
            out = jnp.concatenate(halves, axis=0)
            o_ref[rc * pack_rows:(rc + 1) * pack_rows, cols] = _silu(out).astype(o_ref.dtype)


def gdn_conv_silu(x, w, T):
    N, C = x.shape
    hb = GC_ROWS // GC_HALO
    return pl.pallas_call(
        functools.partial(_gconv_kernel, tiles_per_seq=T // GC_ROWS),
        grid=(N // GC_ROWS, C // GC_COLS),
        in_specs=[pl.BlockSpec((GC_ROWS, GC_COLS), lambda i, j: (i, j)),
                  pl.BlockSpec((GC_HALO, GC_COLS), lambda i, j: (jnp.maximum(i * hb - 1, 0), j)),
                  pl.BlockSpec((GDN_CONV, GC_COLS), lambda i, j: (0, j))],
        out_specs=pl.BlockSpec((GC_ROWS, GC_COLS), lambda i, j: (i, j)),
        out_shape=jax.ShapeDtypeStruct((N, C), BF16),
        scratch_shapes=[pltpu.VMEM((GC_COLS // LANES, DW_PITCH * (GC_HALO + GC_ROWS), LANES), F32)],
        compiler_params=_cparams(("arbitrary", "arbitrary")), name="gdn_conv_silu",
    )(x, x, w)


GP_ROWS = 512
GP_SUB = 256


def _l2norm(t):
    return t * lax.rsqrt(jnp.sum(t * t, axis=-1, keepdims=True) + 1e-6)


def _gdn_prep_kernel(q_ref, k_ref, v_ref, braw_ref, araw_ref, alog_ref, dtb_ref,
                     u_ref, w_ref, qg_ref, kdt_ref, attn_ref, gl_ref):
    C = GDN_CHUNK
    S = GP_SUB
    ri = lax.broadcasted_iota(jnp.int32, (S, S), 0)
    ci = lax.broadcasted_iota(jnp.int32, (S, S), 1)
    same = _shr(ri, C) == _shr(ci, C)
    lower = same & (ri >= ci)
    strict = same & (ri > ci)
    eye = ri == ci
    cum_mat = (same & (ri <= ci)).astype(BF16)
    tot_mat = same.astype(BF16)
    eye_f = eye.astype(F32)

    def to_col(row):
        return jnp.sum(jnp.where(eye, jnp.broadcast_to(row, (S, S)), 0.0), axis=1, keepdims=True)

    n_sb = GP_ROWS // S
    chains = [(sb, e) for sb in range(n_sb) for e in range(2)]
    qs, ks, grams, qks = [], [], [], []
    for sb in range(n_sb):
        rows = slice(sb * S, (sb + 1) * S)
        q = _l2norm(q_ref[rows, :].astype(F32)) * (HEAD_DIM ** -0.5)
        k = _l2norm(k_ref[rows, :].astype(F32))
        kb16 = k.astype(BF16)
        qs.append(q)
        ks.append(k)
        grams.append(_dot_nt(kb16, kb16))
        qks.append(_dot_nt(q.astype(BF16), kb16))

    gcums, gtots, betas, decays, invs, pws = [], [], [], [], [], []
    for sb, e in chains:
        rows = slice(sb * S, (sb + 1) * S)
        a_row = araw_ref[e, :, rows]
        b_row = braw_ref[e, :, rows]
        g_row = -jnp.exp(alog_ref[e, :, 0:1]) * _softplus(a_row + dtb_ref[e, :, 0:1])
        g8 = jnp.broadcast_to(g_row, (SUBLANES, S))
        gcum_row = _dot_x3(g8, cum_mat)[0:1, :]
        gcum = to_col(gcum_row)
        gtots.append(to_col(_dot_x3(g8, tot_mat)[0:1, :]))
        beta = to_col(_sigmoid(b_row))
        decay = jnp.exp(jnp.where(lower, gcum - gcum_row, NEG_INF))
        a = jnp.where(strict, beta * grams[sb] * decay, 0.0)
        gcums.append(gcum)
        betas.append(beta)
        decays.append(decay)
        invs.append(eye_f - a)
        pws.append(a)

    for _ in range(int(math.log2(C)) - 1):
        pw16 = [pw.astype(BF16) for pw in pws]
        pws = [_dot(p16, p16) for p16 in pw16]
        invs = [inv + _dot(inv.astype(BF16), pw.astype(BF16)) for inv, pw in zip(invs, pws)]

    for ci_, (sb, e) in enumerate(chains):
        rows = slice(sb * S, (sb + 1) * S)
        cols = slice(e * HEAD_DIM, (e + 1) * HEAD_DIM)
        q, k = qs[sb], ks[sb]
        gcum, gtot, beta = gcums[ci_], gtots[ci_], betas[ci_]
        eg = jnp.exp(gcum)
        v = v_ref[rows, cols].astype(F32)
        inv16 = invs[ci_].astype(BF16)
        u_ref[rows, cols] = _dot(inv16, (v * beta).astype(BF16))
        w_ref[rows, cols] = _dot(inv16, (k * (beta * eg)).astype(BF16)).astype(BF16)
        qg_ref[rows, cols] = (q * eg).astype(BF16)
        kdec = k * jnp.exp(gtot - gcum)
        kdt_ref[e, :, rows] = kdec.T.astype(BF16)
        attn = qks[sb] * decays[ci_]
        for c in range(S // C):
            attn_ref[e, sb * S + c * C:sb * S + (c + 1) * C, :] = (
                attn[c * C:(c + 1) * C, c * C:(c + 1) * C].astype(BF16))
            gl_ref[e, sb * (S // C) + c:sb * (S // C) + c + 1, :] = jnp.broadcast_to(
                jnp.exp(gtot[c * C:c * C + 1, :]), (1, LANES))


def gdn_prep(qkv, baT, a_log, dt_bias, B, T):
    N = qkv.shape[0]
    HK, HV, dh = GDN_QK_HEADS, GDN_V_HEADS, HEAD_DIM
    nt = T // GP_ROWS
    nc = GP_ROWS // GDN_CHUNK
    rowmap = lambda b, hk, t: (b * nt + t)
    alog = jnp.broadcast_to(a_log.astype(F32)[:, None, None], (HV, 1, LANES))
    dtb = jnp.broadcast_to(dt_bias.astype(F32)[:, None, None], (HV, 1, LANES))
    out_shape = [jax.ShapeDtypeStruct((N, HV * dh), F32),
                 jax.ShapeDtypeStruct((N, HV * dh), BF16),
                 jax.ShapeDtypeStruct((N, HV * dh), BF16),
                 jax.ShapeDtypeStruct((B, HV, dh, T), BF16),
                 jax.ShapeDtypeStruct((B, HV, T, GDN_CHUNK), BF16),
                 jax.ShapeDtypeStruct((B, HV, T // GDN_CHUNK, LANES), F32)]
    big = pl.BlockSpec((GP_ROWS, 2 * dh), lambda b, hk, t: (rowmap(b, hk, t), hk))
    return pl.pallas_call(
        _gdn_prep_kernel,
        grid=(B, HK, nt),
        in_specs=[pl.BlockSpec((GP_ROWS, dh), lambda b, hk, t: (rowmap(b, hk, t), hk)),
                  pl.BlockSpec((GP_ROWS, dh), lambda b, hk, t: (rowmap(b, hk, t), HK + hk)),
                  pl.BlockSpec((GP_ROWS, 2 * dh), lambda b, hk, t: (rowmap(b, hk, t), HK + hk)),
                  pl.BlockSpec((None, 2, 1, GP_ROWS), lambda b, hk, t: (b, hk, 0, t)),
                  pl.BlockSpec((None, 2, 1, GP_ROWS), lambda b, hk, t: (b, HK + hk, 0, t)),
                  pl.BlockSpec((2, 1, LANES), lambda b, hk, t: (hk, 0, 0)),
                  pl.BlockSpec((2, 1, LANES), lambda b, hk, t: (hk, 0, 0))],
        out_specs=[big, big, big,
                   pl.BlockSpec((None, 2, dh, GP_ROWS), lambda b, hk, t: (b, hk, 0, t)),
                   pl.BlockSpec((None, 2, GP_ROWS, GDN_CHUNK), lambda b, hk, t: (b, hk, t, 0)),
                   pl.BlockSpec((None, 2, nc, LANES), lambda b, hk, t: (b, hk, t, 0))],
        out_shape=out_shape,
        compiler_params=_cparams(("arbitrary", "arbitrary", "arbitrary")), name="gdn_prep",
    )(qkv, qkv, qkv, baT, baT, alog, dtb)


GS_HEADS = 4
GS_ROWS = 512


def _gdn_scan_kernel(u_ref, w_ref, qg_ref, kdt_ref, attn_ref, gl_ref, z_ref, nw_ref, o_ref, s_ref):
    C = GDN_CHUNK

    @pl.when(pl.program_id(2) == 0)
    def _():
        s_ref[...] = jnp.zeros_like(s_ref)

    nw = nw_ref[...]
    for c in range(GS_ROWS // C):
        rows = slice(c * C, (c + 1) * C)
        for hh in range(GS_HEADS):
            cols = slice(hh * HEAD_DIM, (hh + 1) * HEAD_DIM)
            s = s_ref[hh]
            lhs = jnp.concatenate([w_ref[rows, cols], qg_ref[rows, cols]], axis=0)
            r = _dot(lhs, s.astype(BF16))
            v_new = u_ref[rows, cols] - r[:C]
            lhs2 = jnp.concatenate([attn_ref[hh, rows, :], kdt_ref[hh, :, rows]], axis=0)
            r2 = _dot(lhs2, v_new.astype(BF16))
            o = r[C:] + r2[:C]
            s_ref[hh] = s * gl_ref[hh, c:c + 1, :] + r2[C:]
            z = z_ref[rows, cols].astype(F32)
            o = o * lax.rsqrt(jnp.mean(o * o, axis=-1, keepdims=True) + 1e-6) * nw * _silu(z)
            o_ref[rows, cols] = o.astype(o_ref.dtype)


def gdn_scan(u, w, qg, kdt, attn, gl, z, norm_w, B, T):
    N = u.shape[0]
    HV, dh = GDN_V_HEADS, HEAD_DIM
    nt = T // GS_ROWS
    nc = GS_ROWS // GDN_CHUNK
    wide = pl.BlockSpec((GS_ROWS, GS_HEADS * dh), lambda b, g, t: (b * nt + t, g))
    return pl.pallas_call(
        _gdn_scan_kernel,
        grid=(B, HV // GS_HEADS, nt),
        in_specs=[wide, wide, wide,
                  pl.BlockSpec((None, GS_HEADS, dh, GS_ROWS), lambda b, g, t: (b, g, 0, t)),
                  pl.BlockSpec((None, GS_HEADS, GS_ROWS, GDN_CHUNK), lambda b, g, t: (b, g, t, 0)),
                  pl.BlockSpec((None, GS_HEADS, nc, LANES), lambda b, g, t: (b, g, t, 0)),
                  wide,
                  pl.BlockSpec((1, dh), lambda b, g, t: (0, 0))],
        out_specs=wide,
        out_shape=jax.ShapeDtypeStruct((N, HV * dh), BF16),
        scratch_shapes=[pltpu.VMEM((GS_HEADS, dh, dh), F32)],
        compiler_params=_cparams(("arbitrary", "arbitrary", "arbitrary")), name="gdn_scan",
    )(u, w, qg, kdt, attn, gl, z, norm_w.astype(F32).reshape(1, dh))


def gdn_mixer(h, w_in, conv_w, a_log, dt_bias, norm_w, w_out, B, T):
    N, D = h.shape
    qkv_dim = 2 * GDN_QK_HEADS * HEAD_DIM + GDN_V_HEADS * HEAD_DIM
    v_dim = GDN_V_HEADS * HEAD_DIM
    w_qkv = w_in[:, :qkv_dim].astype(BF16)
    w_z = w_in[:, qkv_dim:qkv_dim + v_dim].astype(BF16)
    w_ba = jnp.zeros((D, LANES), BF16).at[:, :2 * GDN_V_HEADS].set(w_in[:, qkv_dim + v_dim:].astype(BF16))
    qkv = matmul(h, w_qkv, BF16, name="gdn_in_qkv")
    z = matmul(h, w_z, BF16, name="gdn_in_z")
    ba = matmul(h, w_ba, F32, name="gdn_in_ba")
    qkv = gdn_conv_silu(qkv, conv_w, T)
    baT = ba[:, :2 * GDN_V_HEADS].reshape(B, T, 2 * GDN_V_HEADS).transpose(0, 2, 1)[:, :, None, :]
    u, w, qg, kdt, attn, gl = gdn_prep(qkv, baT, a_log, dt_bias, B, T)
    o = gdn_scan(u, w, qg, kdt, attn, gl, z, norm_w, B, T)
    return matmul(o, w_out.astype(BF16), F32, name="gdn_out")


def _rel_bucket(dist):
    n = jnp.maximum(dist, 0)
    large = REL_EXACT + (jnp.log(jnp.maximum(n, 1).astype(F32) / REL_EXACT)
                         / math.log(REL_MAX_DIST / REL_EXACT) * (REL_BUCKETS - REL_EXACT)).astype(jnp.int32)
    return jnp.where(n < REL_EXACT, n, jnp.minimum(large, REL_BUCKETS - 1))


N_TOEP = 9
BIAS_PAD = 144


def _toeplitz(rows_rev):
    lead = rows_rev.shape[:-1]
    n = Q_BLOCK
    t = jnp.tile(rows_rev, (1,) * len(lead) + (n,))[..., :n * 2 * n].reshape(lead + (n, 2 * n))
    return t[..., n:]


def _nsa_bias_tables(rel_bias):
    n_max = Q_BLOCK * N_TOEP + BIAS_PAD
    dist = jnp.arange(-BIAS_PAD, n_max + 1)
    far = rel_bias.astype(F32)[REL_BUCKETS - 1]
    ftab = ((rel_bias.astype(F32)[_rel_bucket(dist)] - far[None, :]) * LOG2E).T

    def family(shift, n_d):
        rows = []
        for d in range(n_d):
            c = Q_BLOCK * d + Q_BLOCK + shift + BIAS_PAD
            rows.append(ftab[:, c - 2 * Q_BLOCK:c + 1][:, ::-1])
        return _toeplitz(jnp.stack(rows, axis=1))

    toep = family(0, N_TOEP)
    toep = toep.reshape(NSA_GROUPS, NSA_HPG, N_TOEP, Q_BLOCK, Q_BLOCK).transpose(0, 2, 1, 3, 4)
    toep = toep.reshape(NSA_GROUPS, N_TOEP, NSA_HPG * Q_BLOCK, Q_BLOCK)
    cm = family(-CMP_STRIDE, N_TOEP - 1)[..., ::CMP_STRIDE]
    strip = jnp.concatenate([cm[:, d] for d in range(N_TOEP - 2, -1, -1)], axis=-1)
    strip_hi = strip.astype(BF16)
    strip_lo = (strip - strip_hi.astype(F32)).astype(BF16)
    cmpb = jnp.concatenate([strip_hi, strip_lo], axis=-1).reshape(NSA_GROUPS, NSA_HPG * Q_BLOCK, LANES)
    return toep, cmpb


def _compress_kernel(x_ref, pos_ref, w1a_ref, w1b_ref, w2_ref, o_ref):
    x = x_ref[...].astype(F32)
    pos = pos_ref[...]
    a = _dot((x + pos[0:1, :]).astype(BF16), w1a_ref[...])
    b = _dot((x + pos[1:2, :]).astype(BF16), w1b_ref[...])
    n = x.shape[0]
    hid = a + pltpu.roll(b, n - 1, axis=0)
    o_ref[...] = _dot(_silu(hid).astype(BF16), w2_ref[...]).astype(o_ref.dtype)


def nsa_compress(kv_cmp, cmp_pos, w1k, w2k, w1v, w2v, B, T):
    G, dh = NSA_GROUPS, HEAD_DIM
    half = CMP_BLOCK // 2
    nch = T // half
    x = kv_cmp.reshape(B, nch, half, 2, G, dh).transpose(3, 0, 4, 1, 2, 5).reshape(2, B, G, nch, half * dh)
    pos = cmp_pos.astype(F32).reshape(2, 2, half * dh)
    w1 = jnp.stack([w1k, w1v]).astype(BF16)
    w2 = jnp.stack([w2k, w2v]).astype(BF16)
    return pl.pallas_call(
        _compress_kernel,
        grid=(2, B, G),
        in_specs=[pl.BlockSpec((None, None, None, nch, half * dh), lambda s, b, g: (s, b, g, 0, 0)),
                  pl.BlockSpec((None, 2, half * dh), lambda s, b, g: (s, 0, 0)),
                  pl.BlockSpec((None, half * dh, CMP_HIDDEN), lambda s, b, g: (s, 0, 0)),
                  pl.BlockSpec((None, half * dh, CMP_HIDDEN), lambda s, b, g: (s, 1, 0)),
                  pl.BlockSpec((None, CMP_HIDDEN, dh), lambda s, b, g: (s, 0, 0))],
        out_specs=pl.BlockSpec((None, None, None, nch, dh), lambda s, b, g: (s, b, g, 0, 0)),
        out_shape=jax.ShapeDtypeStruct((2, B, G, nch, dh), BF16),
        compiler_params=_cparams(("arbitrary", "arbitrary", "arbitrary")), name="nsa_compress",
    )(x, pos, w1, w1, w2)


SLC_TILE = 512
SM_ROWS = 64
WIN_KEYS = WINDOW + Q_BLOCK


def _nsa_attn_kernel(q_ref, kc_ref, vc_ref, ksa_ref, vs_ref, kw_ref, vw_ref, gate_ref, toep_ref, cmpb_ref,
                     ovl_ref, o_ref, qa_ref, s0_ref, s1_ref, p_ref, al_ref, oc_ref, ow_ref, acc_ref, m_ref, l_ref):
    qi = pl.program_id(2)
    qs = qi * Q_BLOCK
    HPG, dh, QB = NSA_HPG, HEAD_DIM, Q_BLOCK
    R = HPG * QB
    q = q_ref[...]
    Q = jnp.concatenate([q[:, h * dh:(h + 1) * dh] for h in range(HPG)], axis=0)
    tq = qs + lax.broadcasted_iota(jnp.int32, (QB, 1), 0)

    n_cmp = kc_ref.shape[0]
    strip_pos = lax.broadcasted_iota(jnp.int32, (n_cmp, LANES), 0) - 8 * (qi - (N_TOEP - 2))
    strip_lane = lax.broadcasted_iota(jnp.int32, (n_cmp, LANES), 1) & (LANES // 2 - 1)
    kca = jnp.concatenate([kc_ref[...], (strip_pos == strip_lane).astype(BF16)], axis=1)
    s_all = _dot_nt(jnp.concatenate([Q, cmpb_ref[...]], axis=1), kca)
    cidx = lax.broadcasted_iota(jnp.int32, (1, n_cmp), 1)
    valid = (cidx * CMP_STRIDE + (CMP_BLOCK - 1)) <= tq
    p4 = None
    pn16 = []
    for h in range(HPG):
        rows = slice(h * QB, (h + 1) * QB)
        s = jnp.where(valid, s_all[rows], NEG_INF)
        m = jnp.max(s, axis=1, keepdims=True)
        p = jnp.where(valid, jnp.exp2(s - m), 0.0)
        l = jnp.sum(p, axis=1, keepdims=True)
        pn = p * jnp.where(l > 0.0, 1.0 / jnp.where(l > 0.0, l, 1.0), 0.0)
        p4 = pn if p4 is None else p4 + pn
        pn16.append(pn.astype(BF16))
    oc_ref[...] = _dot(jnp.concatenate(pn16, axis=0), vc_ref[...])

    imp = _dot_x2(p4, ovl_ref[...])
    n_slc = imp.shape[1]
    blk = lax.broadcasted_iota(jnp.int32, (QB, n_slc), 1)
    cur = _shr(tq, SLC_BLOCK)
    forced = (blk == 0) | (blk == cur) | (blk == cur - 1)
    work = jnp.where(forced, FORCE_SCORE, jnp.where(blk <= cur, imp, -1.0))
    work = work.T
    sidx = lax.broadcasted_iota(jnp.int32, (n_slc, QB), 0).astype(F32)
    sel = jnp.zeros((n_slc, QB), F32)
    for _ in range(min(SLC_TOP_N, n_slc)):
        mx = jnp.max(work, axis=0, keepdims=True)
        first = jnp.min(jnp.where(work == mx, sidx, float(n_slc)), axis=0, keepdims=True)
        hit = sidx == first
        sel = jnp.where(hit, 1.0, sel)
        work = jnp.where(hit, -2.0, work)
    selneg = jnp.where(sel > 0.5, 0.0, NEG_INF).T.astype(BF16)
    if n_slc < LANES:
        selneg = jnp.concatenate([selneg, jnp.zeros((QB, LANES - n_slc), BF16)], axis=1)
    qa_ref[:, 0:dh] = Q
    for h in range(HPG):
        qa_ref[h * QB:(h + 1) * QB, dh:] = selneg

    m_ref[...] = jnp.full(m_ref.shape, NEG_INF, F32)
    l_ref[...] = jnp.zeros(l_ref.shape, F32)
    acc_ref[...] = jnp.zeros(acc_ref.shape, F32)
    start = pl.multiple_of(jnp.maximum(qs - WINDOW, 0), QB)
    kwin = kw_ref[pl.ds(start, WIN_KEYS), :]
    vwin = vw_ref[pl.ds(start, WIN_KEYS), :]
    sw_all = _dot_nt(Q, kwin)
    doff = (qs - start) // QB
    n_wt = WIN_KEYS // QB
    wmask = []
    for t in range(n_wt):
        dist = tq - (start + t * QB + lax.broadcasted_iota(jnp.int32, (1, QB), 1))
        wmask.append(jnp.where((dist >= 0) & (dist < WINDOW), 0.0, NEG_INF))
    pw16 = []
    inv_lw = []
    for h in range(HPG):
        rows = slice(h * QB, (h + 1) * QB)
        parts = [sw_all[rows, t * QB:(t + 1) * QB] + toep_ref[jnp.clip(doff - t, 0, N_TOEP - 1), rows, :]
                 + wmask[t] for t in range(n_wt)]
        mw = parts[0]
        for t in range(1, n_wt):
            mw = jnp.maximum(mw, parts[t])
        mw = jnp.max(mw, axis=1, keepdims=True)
        pw = [jnp.exp2(pt - mw) for pt in parts]
        lw = pw[0]
        for t in range(1, n_wt):
            lw = lw + pw[t]
        inv_lw.append(1.0 / jnp.sum(lw, axis=1, keepdims=True))
        pw16.append(jnp.concatenate(pw, axis=1).astype(BF16))
    ow_ref[...] = _dot(jnp.concatenate(pw16, axis=0), vwin) * jnp.concatenate(inv_lw, axis=0)

    quarters = SLC_TILE // QB
    n_lane_tiles = SLC_TILE // LANES
    n_tiles = qi // quarters + 1
    last_tile = ksa_ref.shape[0] // SLC_TILE - 1
    j_near = jnp.maximum((qi - (N_TOEP - 2)) // quarters, 0)

    def scores(j, s_ref):
        k0 = pl.multiple_of(jnp.minimum(j, last_tile) * SLC_TILE, SLC_TILE)
        s_ref[...] = _dot_nt(qa_ref[...], ksa_ref[pl.ds(k0, SLC_TILE), :])

    def add_bias(j, s_ref):
        d0 = qi - quarters * j
        kpos = j * SLC_TILE + lax.broadcasted_iota(jnp.int32, (1, SLC_TILE), 1)
        causal = jnp.where(kpos <= tq, 0.0, NEG_INF)
        for h in range(HPG):
            rows = slice(h * QB, (h + 1) * QB)
            delta = jnp.concatenate(
                [toep_ref[jnp.clip(d0 - t, 0, N_TOEP - 1), rows, :] for t in range(quarters)], axis=1)
            s_ref[rows, :] = s_ref[rows, :] + delta + causal

    def softmax_pv(j, s_ref):
        for c in range(R // SM_ROWS):
            rows = slice(c * SM_ROWS, (c + 1) * SM_ROWS)
            parts = [s_ref[rows, t * LANES:(t + 1) * LANES] for t in range(n_lane_tiles)]
            mc = parts[0]
            for t in range(1, n_lane_tiles):
                mc = jnp.maximum(mc, parts[t])
            m_prev = m_ref[rows]
            m_new = jnp.maximum(m_prev, jnp.max(mc, axis=1, keepdims=True))
            alpha = jnp.exp2(m_prev - m_new)
            ps = [jnp.exp2(pt - m_new) for pt in parts]
            row_sum = ps[0]
            for t in range(1, n_lane_tiles):
                row_sum = row_sum + ps[t]
            l_ref[rows] = alpha * l_ref[rows] + jnp.sum(row_sum, axis=1, keepdims=True)
            m_ref[rows] = m_new
            al_ref[rows] = alpha
            for t in range(n_lane_tiles):
                p_ref[rows, t * LANES:(t + 1) * LANES] = ps[t].astype(BF16)
        k0 = pl.multiple_of(j * SLC_TILE, SLC_TILE)
        acc_ref[...] = al_ref[...] * acc_ref[...] + _dot(p_ref[...], vs_ref[pl.ds(k0, SLC_TILE), :])

    scores(0, s0_ref)

    def slc_pair(i, carry):
        ja = 2 * i
        jb = ja + 1

        @pl.when(ja >= j_near)
        def _():
            add_bias(ja, s0_ref)

        scores(jb, s1_ref)
        softmax_pv(ja, s0_ref)

        @pl.when((jb < n_tiles) & (jb >= j_near))
        def _():
            add_bias(jb, s1_ref)

        @pl.when(jb < n_tiles)
        def _():
            scores(jb + 1, s0_ref)
            softmax_pv(jb, s1_ref)

        return carry

    lax.fori_loop(0, (n_tiles + 1) // 2, slc_pair, 0)
    o_slc = acc_ref[...] / l_ref[...]

    gt = _sigmoid(gate_ref[...])
    o_cmp = oc_ref[...]
    o_win = ow_ref[...]
    outs = []
    for h in range(HPG):
        rows = slice(h * QB, (h + 1) * QB)
        outs.append(gt[:, 3 * h:3 * h + 1] * o_cmp[rows] + gt[:, 3 * h + 1:3 * h + 2] * o_slc[rows]
                    + gt[:, 3 * h + 2:3 * h + 3] * o_win[rows])
    o_ref[...] = jnp.concatenate(outs, axis=1).astype(o_ref.dtype)


def nsa_attention(q, kv, kvc, gates, toep, cmpb, B, T):
    N = q.shape[0]
    G, HPG, dh, QB = NSA_GROUPS, NSA_HPG, HEAD_DIM, Q_BLOCK
    nq = T // QB
    n_cmp = T // CMP_STRIDE
    n_slc = T // SLC_BLOCK
    c = jnp.arange(n_cmp)[:, None] * CMP_STRIDE
    sblk = jnp.arange(n_slc)[None, :] * SLC_BLOCK
    ovl = ((c < sblk + SLC_BLOCK) & (c + CMP_BLOCK > sblk)).astype(BF16)

    assert n_slc <= LANES
    member = (jnp.arange(T)[:, None] // SLC_BLOCK == jnp.arange(LANES)[None, :]).astype(BF16)
    k_slc = kv[:, :, 2 * G * dh:3 * G * dh].reshape(B, T, G, dh).transpose(0, 2, 1, 3)
    ksa = jnp.concatenate([k_slc, jnp.broadcast_to(member, (B, G, T, LANES))], axis=-1)

    def kv_spec(branch, which):
        col = (branch * 2 + which) * G
        return pl.BlockSpec((None, T, dh), lambda b, g, qi: (b, 0, col + g))

    def cmp_spec(which):
        return pl.BlockSpec((None, None, None, n_cmp, dh), lambda b, g, qi: (which, b, g, 0, 0))

    R = HPG * QB
    return pl.pallas_call(
        _nsa_attn_kernel,
        grid=(B, G, nq),
        in_specs=[pl.BlockSpec((QB, HPG * dh), lambda b, g, qi: (b * nq + qi, g)),
                  cmp_spec(0), cmp_spec(1),
                  pl.BlockSpec((None, None, T, dh + LANES), lambda b, g, qi: (b, g, 0, 0)),
                  kv_spec(1, 1), kv_spec(2, 0), kv_spec(2, 1),
                  pl.BlockSpec((QB, LANES), lambda b, g, qi: (b * nq + qi, g)),
                  pl.BlockSpec((None, N_TOEP, R, QB), lambda b, g, qi: (g, 0, 0, 0)),
                  pl.BlockSpec((None, R, LANES), lambda b, g, qi: (g, 0, 0)),
                  pl.BlockSpec((n_cmp, n_slc), lambda b, g, qi: (0, 0))],
        out_specs=pl.BlockSpec((QB, HPG * dh), lambda b, g, qi: (b * nq + qi, g)),
        out_shape=jax.ShapeDtypeStruct((N, G * HPG * dh), BF16),
        scratch_shapes=[pltpu.VMEM((R, dh + LANES), BF16),
                        pltpu.VMEM((R, SLC_TILE), F32),
                        pltpu.VMEM((R, SLC_TILE), F32),
                        pltpu.VMEM((R, SLC_TILE), BF16),
                        pltpu.VMEM((R, LANES), F32),
                        pltpu.VMEM((R, dh), F32),
                        pltpu.VMEM((R, dh), F32),
                        pltpu.VMEM((R, dh), F32), pltpu.VMEM((R, LANES), F32), pltpu.VMEM((R, LANES), F32)],
        compiler_params=_cparams(("arbitrary", "arbitrary", "arbitrary"), vmem_mib=56), name="nsa_attention",
    )(q, kvc, kvc, ksa, kv, kv, kv, gates, toep, cmpb, ovl)


def nsa_mixer(h, w_in, cmp_pos, w1k, w2k, w1v, w2v, w_o, toep, cmpb, B, T):
    N, D = h.shape
    G, HPG, dh = NSA_GROUPS, NSA_HPG, HEAD_DIM
    q_dim = NSA_HEADS * dh
    kv_dim = 3 * 2 * G * dh
    wq = w_in[:, :q_dim].astype(BF16)
    wkv = w_in[:, q_dim:q_dim + kv_dim].astype(BF16)
    wg = w_in[:, q_dim + kv_dim:].astype(BF16).reshape(D, G, HPG * 3)
    wg = jnp.zeros((D, G, LANES), BF16).at[:, :, :HPG * 3].set(wg).reshape(D, G * LANES)
    q = matmul(h, wq, BF16, scale=dh ** -0.5 * LOG2E, name="nsa_in_q")
    kv = matmul(h, wkv, BF16, name="nsa_in_kv")
    gates = matmul(h, wg, F32, name="nsa_in_gate")
    kv3 = kv.reshape(B, T, kv_dim)
    kvc = nsa_compress(kv3[:, :, :2 * G * dh], cmp_pos, w1k, w2k, w1v, w2v, B, T)
    o = nsa_attention(q, kv3, kvc, gates, toep, cmpb, B, T)
    return matmul(o, w_o.astype(BF16), F32, name="nsa_out")


def kernel(x, c, rel_bias, ada_w, ada_b, ln_g, ln_b, nsa_w_in, nsa_cmp_pos, nsa_cmp_w1k, nsa_cmp_w2k, nsa_cmp_w1v, nsa_cmp_w2v, nsa_w_o, conv_w_in, conv_dw, conv_ln_g, conv_ln_b, conv_w_out, gdn_w_in, gdn_conv, gdn_a_log, gdn_dt_bias, gdn_norm_w, gdn_w_out, ffn_w1, ffn_w3, ffn_w2, moe_router, moe_w1, moe_w3, moe_w2):
    B, T, D = x.shape
    N = B * T
    depth = ada_w.shape[0]
    mod = adaln_all(c, ada_w, ada_b).reshape(depth, B, 6, 1, D)
    toep, cmpb = _nsa_bias_tables(rel_bias)
    x2 = x.reshape(N, D)

    def h_dtype_for(layer, sub):
        return F32 if (sub == 1 and layer % 2 == 1) else BF16

    h = modulate(x2, mod[0, :, 1], mod[0, :, 0], T, h_dtype_for(0, 0))
    for i in range(depth):
        sh1, sc1, g1, sh2, sc2, g2 = (mod[i, :, k] for k in range(6))
        kind, j = i % 3, i // 3
        if kind == 0:
            y = nsa_mixer(h, nsa_w_in[j], nsa_cmp_pos[j], nsa_cmp_w1k[j], nsa_cmp_w2k[j], nsa_cmp_w1v[j],
                          nsa_cmp_w2v[j], nsa_w_o[j], toep, cmpb, B, T)
        elif kind == 1:
            y = conformer_mixer(h, conv_w_in[j], conv_dw[j], conv_ln_g[j], conv_ln_b[j], conv_w_out[j], T)
        else:
            y = gdn_mixer(h, gdn_w_in[j], gdn_conv[j], gdn_a_log[j], gdn_dt_bias[j], gdn_norm_w[j],
                          gdn_w_out[j], B, T)
        lg = ln_g[i].reshape(2, 1, D)
        lb = ln_b[i].reshape(2, 1, D)
        x2, h = norm_mod(x2, y, g1, lg[0], lb[0], sc2, sh2, T, h_dtype_for(i, 1))
        m = i // 2
        last = i == depth - 1
        sc_n = None if last else mod[i + 1, :, 1]
        sh_n = None if last else mod[i + 1, :, 0]
        if i % 2 == 0:
            y = ffn_swiglu(h, ffn_w1, ffn_w3, ffn_w2, m)
            x2, h = norm_mod(x2, y, g2, lg[1], lb[1], sc_n, sh_n, T, h_dtype_for(i + 1, 0))
        else:
            x2, h = moe_sublayer(h, x2, moe_router[m], moe_w1, moe_w3, moe_w2, m,
                                 g2, lg[1], lb[1], sc_n, sh_n, T, h_dtype_for(i + 1, 0))
    return x2.reshape(B, T, D)
```

```python
import functools
import math

import jax
import jax.numpy as jnp
from jax import lax
from jax.experimental import pallas as pl
from jax.experimental.pallas import tpu as pltpu

F32 = jnp.float32
BF16 = jnp.bfloat16

DEPTH = 4
ALPHA = (2 * DEPTH) ** 0.25
LN_EPS = 1e-5
NEG_INF = -1e30
LOG2E = math.log2(math.e)

REL_BUCKETS = 32
REL_EXACT = 16
REL_MAX_DIST = 1024

NSA_HEADS = 16
NSA_GROUPS = 4
NSA_HPG = NSA_HEADS // NSA_GROUPS
HEAD_DIM = 128
CMP_BLOCK = 32
CMP_STRIDE = 16
CMP_HIDDEN = 256
SLC_BLOCK = 64
SLC_TOP_N = 16
WINDOW = 512
Q_BLOCK = 128

CONV_WIDTH = 31
GDN_QK_HEADS = 16
GDN_V_HEADS = 32
GDN_CONV = 4
GDN_CHUNK = 64

N_EXPERTS = 8
TOP_K = 2

LANES = 128
SUBLANES = 8
MIB = 1 << 20


def _cparams(sem, vmem_mib=48):
    return pltpu.CompilerParams(dimension_semantics=sem, vmem_limit_bytes=vmem_mib * MIB)


def _dot(a, b):
    return jnp.dot(a, b, preferred_element_type=F32)


def _dot_nt(a, b):
    return lax.dot_general(a, b, (((1,), (1,)), ((), ())), preferred_element_type=F32)


def _split3(a):
    hi = a.astype(BF16)
    r1 = a - hi.astype(F32)
    mid = r1.astype(BF16)
    lo = (r1 - mid.astype(F32)).astype(BF16)
    return hi, mid, lo


def _dot_x3(a, b_bf16):
    hi, mid, lo = _split3(a)
    return _dot(hi, b_bf16) + _dot(mid, b_bf16) + _dot(lo, b_bf16)


def _dot_x2(a, b_bf16):
    hi = a.astype(BF16)
    lo = (a - hi.astype(F32)).astype(BF16)
    return _dot(hi, b_bf16) + _dot(lo, b_bf16)


def _silu(x):
    return x * (1.0 / (1.0 + jnp.exp(-x)))


def _sigmoid(x):
    return 1.0 / (1.0 + jnp.exp(-x))


def _softplus(x):
    return jnp.maximum(x, 0.0) + jnp.log(1.0 + jnp.exp(-jnp.abs(x)))


def _shr(x, pow2):
    return lax.shift_right_logical(x, jnp.full(x.shape, int(math.log2(pow2)), jnp.int32))


def _adaln_kernel(ct_ref, w_ref, b_ref, o_ref):
    w = w_ref[...]
    tn = w.shape[1]
    rows = []
    for b in range(ct_ref.shape[0]):
        cb = _silu(ct_ref[b])
        parts = [jnp.sum(w[:, j * LANES:(j + 1) * LANES] * cb, axis=0, keepdims=True)
                 for j in range(tn // LANES)]
        rows.append(jnp.concatenate(parts, axis=1))
    o_ref[...] = jnp.concatenate(rows, axis=0) + b_ref[...]


def adaln_all(c, ada_w, ada_b):
    B, D = c.shape
    depth, _, n_out = ada_w.shape
    tn = 1024
    ct = jnp.broadcast_to(c[:, :, None], (B, D, LANES))
    return pl.pallas_call(
        _adaln_kernel,
        grid=(depth, n_out // tn),
        in_specs=[pl.BlockSpec((B, D, LANES), lambda i, j: (0, 0, 0)),
                  pl.BlockSpec((None, D, tn), lambda i, j: (i, 0, j)),
                  pl.BlockSpec((None, 1, tn), lambda i, j: (i, 0, j))],
        out_specs=pl.BlockSpec((None, B, tn), lambda i, j: (i, 0, j)),
        out_shape=jax.ShapeDtypeStruct((depth, B, n_out), F32),
        compiler_params=_cparams(("arbitrary", "arbitrary")),
        name="adaln",
    )(ct, ada_w, ada_b.reshape(depth, 1, n_out))


def _mod_kernel(x_ref, sc_ref, sh_ref, h_ref):
    h_ref[...] = (x_ref[...] * (1.0 + sc_ref[...]) + sh_ref[...]).astype(h_ref.dtype)


def modulate(x2, sc, sh, T, h_dtype):
    N, D = x2.shape
    tm = 512
    vec = pl.BlockSpec((None, 1, D), lambda i: ((i * tm) // T, 0, 0))
    return pl.pallas_call(
        _mod_kernel,
        grid=(N // tm,),
        in_specs=[pl.BlockSpec((tm, D), lambda i: (i, 0)), vec, vec],
        out_specs=pl.BlockSpec((tm, D), lambda i: (i, 0)),
        out_shape=jax.ShapeDtypeStruct((N, D), h_dtype),
        compiler_params=_cparams(("arbitrary",)),
        name="modulate",
    )(x2, sc, sh)


def _deepnorm(x, y, gate, lg, lb):
    z = ALPHA * x + (1.0 + gate) * y
    mu = jnp.mean(z, axis=-1, keepdims=True)
    zc = z - mu
    var = jnp.mean(zc * zc, axis=-1, keepdims=True)
    return zc * lax.rsqrt(var + LN_EPS) * lg + lb


def _norm_mod_kernel(x_ref, y_ref, g_ref, lg_ref, lb_ref, sc_ref, sh_ref, xo_ref, ho_ref):
    xn = _deepnorm(x_ref[...], y_ref[...].astype(F32), g_ref[...], lg_ref[...], lb_ref[...])
    xo_ref[...] = xn
    ho_ref[...] = (xn * (1.0 + sc_ref[...]) + sh_ref[...]).astype(ho_ref.dtype)


def _norm_kernel(x_ref, y_ref, g_ref, lg_ref, lb_ref, xo_ref):
    xo_ref[...] = _deepnorm(x_ref[...], y_ref[...].astype(F32), g_ref[...], lg_ref[...], lb_ref[...])


def norm_mod(x2, y, gate, lg, lb, sc, sh, T, h_dtype):
    N, D = x2.shape
    tm = 256
    row = pl.BlockSpec((tm, D), lambda i: (i, 0))
    vec = pl.BlockSpec((None, 1, D), lambda i: ((i * tm) // T, 0, 0))
    par = pl.BlockSpec((1, D), lambda i: (0, 0))
    if sc is None:
        return pl.pallas_call(
            _norm_kernel, grid=(N // tm,),
            in_specs=[row, row, vec, par, par], out_specs=row,
            out_shape=jax.ShapeDtypeStruct((N, D), F32),
            compiler_params=_cparams(("arbitrary",)), name="deepnorm",
        )(x2, y, gate, lg, lb), None
    return pl.pallas_call(
        _norm_mod_kernel, grid=(N // tm,),
        in_specs=[row, row, vec, par, par, vec, vec], out_specs=[row, row],
        out_shape=[jax.ShapeDtypeStruct((N, D), F32), jax.ShapeDtypeStruct((N, D), h_dtype)],
        compiler_params=_cparams(("arbitrary",)), name="deepnorm_mod",
    )(x2, y, gate, lg, lb, sc, sh)


def _mm_kernel(x_ref, w_ref, o_ref, *, scale):
    acc = _dot(x_ref[...], w_ref[...])
    if scale != 1.0:
        acc = acc * scale
    o_ref[...] = acc.astype(o_ref.dtype)


def matmul(x, w, out_dtype, *, tm=1024, tn=512, scale=1.0, name="matmul"):
    M, K = x.shape
    N = w.shape[1]
    tn = min(tn, N)
    return pl.pallas_call(
        functools.partial(_mm_kernel, scale=scale),
        grid=(M // tm, N // tn),
        in_specs=[pl.BlockSpec((tm, K), lambda i, j: (i, 0)),
                  pl.BlockSpec((K, tn), lambda i, j: (0, j))],
        out_specs=pl.BlockSpec((tm, tn), lambda i, j: (i, j)),
        out_shape=jax.ShapeDtypeStruct((M, N), out_dtype),
        compiler_params=_cparams(("arbitrary", "arbitrary")),
        name=name,
    )(x, w)


def _mm_glu_kernel(x_ref, wa_ref, wb_ref, o_ref):
    x = x_ref[...]
    a = _dot(x, wa_ref[...])
    b = _dot(x, wb_ref[...])
    o_ref[...] = (a * _sigmoid(b)).astype(o_ref.dtype)


def matmul_glu(x, w, out_dtype, *, tm=1024, tn=512):
    M, K = x.shape
    n = w.shape[1] // 2
    nj = n // tn
    return pl.pallas_call(
        _mm_glu_kernel,
        grid=(M // tm, nj),
        in_specs=[pl.BlockSpec((tm, K), lambda i, j: (i, 0)),
                  pl.BlockSpec((K, tn), lambda i, j: (0, j)),
                  pl.BlockSpec((K, tn), lambda i, j: (0, j + nj))],
        out_specs=pl.BlockSpec((tm, tn), lambda i, j: (i, j)),
        out_shape=jax.ShapeDtypeStruct((M, n), out_dtype),
        compiler_params=_cparams(("arbitrary", "arbitrary")),
        name="matmul_glu",
    )(x, w, w)


def _ffn_kernel(x_ref, w1_ref, w3_ref, w2_ref, o_ref):
    f = pl.program_id(1)
    @pl.when(f == 0)
    def _():
        o_ref[...] = jnp.zeros_like(o_ref)

    x = x_ref[...]
    a = _dot(x, w1_ref[...].astype(BF16))
    b = _dot(x, w3_ref[...].astype(BF16))
    o_ref[...] += _dot((_silu(a) * b).astype(BF16), w2_ref[...].astype(BF16))


def ffn_swiglu(h, w1, w3, w2, layer, *, tm=1024, tf=256):
    N, D = h.shape
    Fd = w1.shape[2]
    single = pl.Buffered(1)
    return pl.pallas_call(
        _ffn_kernel,
        grid=(N // tm, Fd // tf),
        in_specs=[pl.BlockSpec((tm, D), lambda i, f: (i, 0), pipeline_mode=single),
                  pl.BlockSpec((None, D, tf), lambda i, f: (layer, 0, f)),
                  pl.BlockSpec((None, D, tf), lambda i, f: (layer, 0, f)),
                  pl.BlockSpec((None, tf, D), lambda i, f: (layer, f, 0))],
        out_specs=pl.BlockSpec((tm, D), lambda i, f: (i, 0), pipeline_mode=single),
        out_shape=jax.ShapeDtypeStruct((N, D), F32),
        compiler_params=_cparams(("arbitrary", "arbitrary")),
        name="ffn_swiglu",
    )(h, w1, w3, w2)


def _router_kernel(h_ref, rh_ref, rl_ref, idx_ref, wt_ref):
    h = h_ref[...]
    hh = h.astype(BF16)
    hl = (h - hh.astype(F32)).astype(BF16)
    rh = rh_ref[...]
    logits = _dot(hh, rh) + _dot(hl, rh) + _dot(hh, rl_ref[...])
    lane = lax.broadcasted_iota(jnp.int32, logits.shape, 1)
    lanef = lane.astype(F32)
    logits = jnp.where(lane < N_EXPERTS, logits, -jnp.inf)
    m1 = jnp.max(logits, axis=1, keepdims=True)
    i1 = jnp.min(jnp.where(logits == m1, lanef, float(LANES)), axis=1, keepdims=True)
    rest = jnp.where(lanef == i1, -jnp.inf, logits)
    m2 = jnp.max(rest, axis=1, keepdims=True)
    i2 = jnp.min(jnp.where(rest == m2, lanef, float(LANES)), axis=1, keepdims=True)
    e2 = jnp.exp(m2 - m1)
    w1 = 1.0 / (1.0 + e2)
    w2 = e2 / (1.0 + e2)
    idx_ref[...] = jnp.where(lane == 0, i1, jnp.where(lane == 1, i2, 0.0)).astype(jnp.int32)
    wt_ref[...] = jnp.where(lane == 0, w1, jnp.where(lane == 1, w2, 0.0))


def moe_router(h, router):
    N, D = h.shape
    tm = 512
    rp = jnp.zeros((D, LANES), F32).at[:, :N_EXPERTS].set(router)
    rh = rp.astype(BF16)
    rl = (rp - rh.astype(F32)).astype(BF16)
    row = pl.BlockSpec((tm, LANES), lambda i: (i, 0))
    return pl.pallas_call(
        _router_kernel, grid=(N // tm,),
        in_specs=[pl.BlockSpec((tm, D), lambda i: (i, 0)),
                  pl.BlockSpec((D, LANES), lambda i: (0, 0)),
                  pl.BlockSpec((D, LANES), lambda i: (0, 0))],
        out_specs=[row, row],
        out_shape=[jax.ShapeDtypeStruct((N, LANES), jnp.int32), jax.ShapeDtypeStruct((N, LANES), F32)],
        compiler_params=_cparams(("arbitrary",)), name="moe_router",
    )(h, rh, rl)


MOE_ROW_GROUPS = 4
DMA_UNROLL = 8


def _row_copy(src_hbm, dst_ref, sem, src_row, dst_row):
    return pltpu.make_async_copy(src_hbm.at[pl.ds(src_row, 1), :], dst_ref.at[pl.ds(dst_row, 1), :], sem)


def _row_put(src_ref, dst_hbm, sem, src_row, dst_row):
    return pltpu.make_async_copy(src_ref.at[pl.ds(src_row, 1), :], dst_hbm.at[pl.ds(dst_row, 1), :], sem)


def _dispatch_rows_kernel(pos_ref, x_ref, init_hbm, o_hbm, sem, *, rows):
    del init_hbm
    base = pl.program_id(0) * rows

    def start(r, carry):
        for k in range(TOP_K):
            _row_put(x_ref, o_hbm, sem, r, pos_ref[TOP_K * (base + r) + k]).start()
        return carry

    def wait(r, carry):
        for k in range(TOP_K):
            _row_put(x_ref, o_hbm, sem, r, pos_ref[TOP_K * (base + r) + k]).wait()
        return carry

    lax.fori_loop(0, rows, start, 0, unroll=DMA_UNROLL)
    lax.fori_loop(0, rows, wait, 0, unroll=DMA_UNROLL)


def dispatch_rows(x, pos, n_out, *, rows=256):
    N, D = x.shape
    return pl.pallas_call(
        functools.partial(_dispatch_rows_kernel, rows=rows),
        grid_spec=pltpu.PrefetchScalarGridSpec(
            num_scalar_prefetch=1, grid=(N // rows,),
            in_specs=[pl.BlockSpec((rows, D), lambda i, p: (i, 0)), pl.BlockSpec(memory_space=pl.ANY)],
            out_specs=pl.BlockSpec(memory_space=pl.ANY),
            scratch_shapes=[pltpu.SemaphoreType.DMA(())]),
        out_shape=jax.ShapeDtypeStruct((n_out, D), x.dtype),
        input_output_aliases={2: 0},
        compiler_params=_cparams(("arbitrary",)), name="moe_dispatch",
    )(pos, x, jnp.zeros((n_out, D), x.dtype))


def _moe_mm_kernel(te_ref, tv_ref, x_ref, w1_ref, w3_ref, w2_ref, o_ref, xb_ref):
    i = pl.program_id(0)
    f = pl.program_id(1)

    @pl.when(f == 0)
    def _():
        xb_ref[...] = x_ref[...].astype(BF16)
        o_ref[...] = jnp.zeros_like(o_ref)

    group = xb_ref.shape[0] // MOE_ROW_GROUPS
    for n_groups in range(1, MOE_ROW_GROUPS + 1):
        @pl.when(tv_ref[i] == n_groups)
        def _(rows=n_groups * group):
            x = xb_ref[0:rows, :]
            a = _dot(x, w1_ref[...].astype(BF16))
            b = _dot(x, w3_ref[...].astype(BF16))
            o_ref[0:rows, :] += _dot((_silu(a) * b).astype(BF16), w2_ref[...].astype(BF16))


def moe_grouped_swiglu(xs, tile_expert, tile_valid, w1, w3, w2, layer, *, tm, tf=256):
    R, D = xs.shape
    Fd = w1.shape[3]
    single = pl.Buffered(1)
    return pl.pallas_call(
        _moe_mm_kernel,
        grid_spec=pltpu.PrefetchScalarGridSpec(
            num_scalar_prefetch=2, grid=(R // tm, Fd // tf),
            in_specs=[pl.BlockSpec((tm, D), lambda i, f, te, tv: (i, 0), pipeline_mode=single),
                      pl.BlockSpec((None, None, D, tf),
                                   lambda i, f, te, tv: (layer, te[i], 0, f * jnp.minimum(tv[i], 1))),
                      pl.BlockSpec((None, None, D, tf),
                                   lambda i, f, te, tv: (layer, te[i], 0, f * jnp.minimum(tv[i], 1))),
                      pl.BlockSpec((None, None, tf, D),
                                   lambda i, f, te, tv: (layer, te[i], f * jnp.minimum(tv[i], 1), 0))],
            out_specs=pl.BlockSpec((tm, D), lambda i, f, te, tv: (i, 0), pipeline_mode=single),
            scratch_shapes=[pltpu.VMEM((tm, D), BF16)]),
        out_shape=jax.ShapeDtypeStruct((R, D), F32),
        compiler_params=_cparams(("arbitrary", "arbitrary"), vmem_mib=52), name="moe_grouped_swiglu",
    )(tile_expert, tile_valid, xs, w1, w3, w2)


def _combine_gather(pos_ref, ys_hbm, buf_ref, sem, base, rows):
    def start(r, carry):
        _row_copy(ys_hbm, buf_ref.at[0], sem, pos_ref[2 * (base + r)], r).start()
        _row_copy(ys_hbm, buf_ref.at[1], sem, pos_ref[2 * (base + r) + 1], r).start()
        return carry

    def wait(r, carry):
        _row_copy(ys_hbm, buf_ref.at[0], sem, pos_ref[2 * (base + r)], r).wait()
        _row_copy(ys_hbm, buf_ref.at[1], sem, pos_ref[2 * (base + r) + 1], r).wait()
        return carry

    lax.fori_loop(0, rows, start, 0, unroll=DMA_UNROLL)
    lax.fori_loop(0, rows, wait, 0, unroll=DMA_UNROLL)


def _moe_combine(wt_ref, buf_ref):
    wt = wt_ref[...]
    return wt[:, 0:1] * buf_ref[0] + wt[:, 1:2] * buf_ref[1]


def _combine_norm_mod_kernel(pos_ref, ys_hbm, wt_ref, x_ref, g_ref, lg_ref, lb_ref, sc_ref, sh_ref,
                             xo_ref, ho_ref, buf_ref, sem, *, rows):
    _combine_gather(pos_ref, ys_hbm, buf_ref, sem, pl.program_id(0) * rows, rows)
    xn = _deepnorm(x_ref[...], _moe_combine(wt_ref, buf_ref), g_ref[...], lg_ref[...], lb_ref[...])
    xo_ref[...] = xn
    ho_ref[...] = (xn * (1.0 + sc_ref[...]) + sh_ref[...]).astype(ho_ref.dtype)


def _combine_norm_kernel(pos_ref, ys_hbm, wt_ref, x_ref, g_ref, lg_ref, lb_ref, xo_ref, buf_ref, sem, *, rows):
    _combine_gather(pos_ref, ys_hbm, buf_ref, sem, pl.program_id(0) * rows, rows)
    xo_ref[...] = _deepnorm(x_ref[...], _moe_combine(wt_ref, buf_ref), g_ref[...], lg_ref[...], lb_ref[...])


def moe_combine_norm_mod(ys, pos, wts, x2, gate, lg, lb, sc, sh, T, h_dtype, *, rows=256):
    N, D = x2.shape
    row = pl.BlockSpec((rows, D), lambda i, p: (i, 0))
    vec = pl.BlockSpec((None, 1, D), lambda i, p: ((i * rows) // T, 0, 0))
    par = pl.BlockSpec((1, D), lambda i, p: (0, 0))
    wsp = pl.BlockSpec((rows, LANES), lambda i, p: (i, 0))
    anysp = pl.BlockSpec(memory_space=pl.ANY)
    scratch = [pltpu.VMEM((2, rows, D), F32), pltpu.SemaphoreType.DMA(())]
    if sc is None:
        out = pl.pallas_call(
            functools.partial(_combine_norm_kernel, rows=rows),
            grid_spec=pltpu.PrefetchScalarGridSpec(
                num_scalar_prefetch=1, grid=(N // rows,),
                in_specs=[anysp, wsp, row, vec, par, par], out_specs=row, scratch_shapes=scratch),
            out_shape=jax.ShapeDtypeStruct((N, D), F32),
            compiler_params=_cparams(("arbitrary",)), name="moe_combine_norm",
        )(pos, ys, wts, x2, gate, lg, lb)
        return out, None
    return pl.pallas_call(
        functools.partial(_combine_norm_mod_kernel, rows=rows),
        grid_spec=pltpu.PrefetchScalarGridSpec(
            num_scalar_prefetch=1, grid=(N // rows,),
            in_specs=[anysp, wsp, row, vec, par, par, vec, vec], out_specs=[row, row],
            scratch_shapes=scratch),
        out_shape=[jax.ShapeDtypeStruct((N, D), F32), jax.ShapeDtypeStruct((N, D), h_dtype)],
        compiler_params=_cparams(("arbitrary",)), name="moe_combine_norm_mod",
    )(pos, ys, wts, x2, gate, lg, lb, sc, sh)


def _route_tables(idx2, tm, n_tiles):
    e_flat = idx2.reshape(-1)
    onehot = (jnp.arange(N_EXPERTS, dtype=jnp.int32)[:, None] == e_flat[None, :]).astype(jnp.int32)
    csum = jnp.cumsum(onehot, axis=1)
    counts = csum[:, -1]
    rank = jnp.sum(onehot * (csum - 1), axis=0)
    tiles_per = (counts + tm - 1) // tm
    tile_end = jnp.cumsum(tiles_per)
    pad_start = (tile_end - tiles_per) * tm
    t = jnp.arange(n_tiles)
    te = jnp.sum((t[:, None] >= tile_end[None, :]).astype(jnp.int32), axis=1)
    tile_expert = jnp.minimum(te, N_EXPERTS - 1).astype(jnp.int32)
    first_tile = (tile_end - tiles_per)[tile_expert]
    used_rows = jnp.where(te < N_EXPERTS, jnp.clip(counts[tile_expert] - (t - first_tile) * tm, 0, tm), 0)
    group = tm // MOE_ROW_GROUPS
    tile_valid = ((used_rows + group - 1) // group).astype(jnp.int32)
    pos = (jnp.sum(onehot * pad_start[:, None], axis=0) + rank).astype(jnp.int32)
    return tile_expert, tile_valid, pos


def moe_sublayer(h, x2, router, w1, w3, w2, layer, gate, lg, lb, sc, sh, T, h_dtype, *, tm=1024):
    N, D = h.shape
    idx, wts = moe_router(h, router)
    n_tiles = (N * TOP_K) // tm + N_EXPERTS
    tile_expert, tile_valid, pos = _route_tables(idx[:, :TOP_K], tm, n_tiles)
    xs = dispatch_rows(h, pos, n_tiles * tm)
    ys = moe_grouped_swiglu(xs, tile_expert, tile_valid, w1, w3, w2, layer, tm=tm)
    return moe_combine_norm_mod(ys, pos, wts, x2, gate, lg, lb, sc, sh, T, h_dtype)


DW_HALO = 32
DW_ROWS = 128
DW_PITCH = 3
DW_GROUP = 4


def _dwconv_kernel(u_ref, halo_ref, dw_ref, lg_ref, lb_ref, o_ref, ext_ref, acc_ref, *, tiles_per_seq):
    i = pl.program_id(0)
    first = (i % tiles_per_seq) == 0
    lead = DW_HALO - (CONV_WIDTH - 1)
    P = DW_PITCH
    lg = lg_ref[...]
    lb = lb_ref[...]
    for sl in range(u_ref.shape[1] // LANES):
        cols = slice(sl * LANES, (sl + 1) * LANES)
        ext_ref[sl, pl.ds(0, DW_HALO, stride=P), :] = jnp.where(first, 0.0, halo_ref[:, cols].astype(F32))
        ext_ref[sl, pl.ds(P * DW_HALO, DW_ROWS, stride=P), :] = u_ref[:, cols].astype(F32)
    for sl in range(u_ref.shape[1] // LANES):
        cols = slice(sl * LANES, (sl + 1) * LANES)
        for rg in range(DW_ROWS // (SUBLANES * DW_GROUP)):
            accs = [None] * DW_GROUP
            for k in range(CONV_WIDTH):
                w = dw_ref[k:k + 1, cols]
                for t in range(DW_GROUP):
                    r0 = (rg * DW_GROUP + t) * SUBLANES
                    term = ext_ref[sl, pl.ds(P * (r0 + lead + k), SUBLANES, stride=P), :] * w
                    accs[t] = term if accs[t] is None else accs[t] + term
            for t in range(DW_GROUP):
                r0 = (rg * DW_GROUP + t) * SUBLANES
                acc_ref[r0:r0 + SUBLANES, cols] = accs[t]
    pack_rows = 2 * SUBLANES
    for rc in range(DW_ROWS // pack_rows):
        r0 = rc * pack_rows
        acc = acc_ref[r0:r0 + pack_rows, :]
        mu = jnp.mean(acc, axis=-1, keepdims=True)
        zc = acc - mu
        var = jnp.mean(zc * zc, axis=-1, keepdims=True)
        v = zc * lax.rsqrt(var + LN_EPS) * lg + lb
        o_ref[r0:r0 + pack_rows, :] = _silu(v).astype(o_ref.dtype)


def dwconv_ln_silu(u, dw, lg, lb, T):
    N, D = u.shape
    hb = DW_ROWS // DW_HALO
    return pl.pallas_call(
        functools.partial(_dwconv_kernel, tiles_per_seq=T // DW_ROWS),
        grid=(N // DW_ROWS,),
        in_specs=[pl.BlockSpec((DW_ROWS, D), lambda i: (i, 0)),
                  pl.BlockSpec((DW_HALO, D), lambda i: (jnp.maximum(i * hb - 1, 0), 0)),
                  pl.BlockSpec((CONV_WIDTH, D), lambda i: (0, 0)),
                  pl.BlockSpec((1, D), lambda i: (0, 0)),
                  pl.BlockSpec((1, D), lambda i: (0, 0))],
        out_specs=pl.BlockSpec((DW_ROWS, D), lambda i: (i, 0)),
        out_shape=jax.ShapeDtypeStruct((N, D), BF16),
        scratch_shapes=[pltpu.VMEM((D // LANES, DW_PITCH * (DW_HALO + DW_ROWS), LANES), F32),
                        pltpu.VMEM((DW_ROWS, D), F32)],
        compiler_params=_cparams(("arbitrary",)), name="dwconv_ln_silu",
    )(u, u, dw, lg.reshape(1, D), lb.reshape(1, D))


def conformer_mixer(h, w_in, dw, lg, lb, w_out, T):
    u = matmul_glu(h, w_in.astype(BF16), BF16)
    v = dwconv_ln_silu(u, dw, lg, lb, T)
    return matmul(v, w_out.astype(BF16), F32, name="conv_out")


GC_HALO = 16
GC_ROWS = 512
GC_COLS = 1024


def _gconv_kernel(x_ref, halo_ref, w_ref, o_ref, ext_ref, *, tiles_per_seq):
    i = pl.program_id(0)
    first = (i % tiles_per_seq) == 0
    lead = GC_HALO - (GDN_CONV - 1)
    P = DW_PITCH
    pack_rows = 2 * SUBLANES
    for sl in range(GC_COLS // LANES):
        cols = slice(sl * LANES, (sl + 1) * LANES)
        ext_ref[sl, pl.ds(0, GC_HALO, stride=P), :] = jnp.where(first, 0.0, halo_ref[:, cols].astype(F32))
        ext_ref[sl, pl.ds(P * GC_HALO, GC_ROWS, stride=P), :] = x_ref[:, cols].astype(F32)
    for sl in range(GC_COLS // LANES):
        cols = slice(sl * LANES, (sl + 1) * LANES)
        taps = [w_ref[k:k + 1, cols] for k in range(GDN_CONV)]
        for rc in range(GC_ROWS // pack_rows):
            halves = []
            for r0 in (rc * pack_rows, rc * pack_rows + SUBLANES):
                acc = None
                for k in range(GDN_CONV):
                    term = ext_ref[sl, pl.ds(P * (r0 + lead + k), SUBLANES, stride=P), :] * taps[k]
                    acc = term if acc is None else acc + term
                halves.append(acc)
            out = jnp.concatenate(halves, axis=0)
            o_ref[rc * pack_rows:(rc + 1) * pack_rows, cols] = _silu(out).astype(o_ref.dtype)


def gdn_conv_silu(x, w, T):
    N, C = x.shape
    hb = GC_ROWS // GC_HALO
    return pl.pallas_call(
        functools.partial(_gconv_kernel, tiles_per_seq=T // GC_ROWS),
        grid=(N // GC_ROWS, C // GC_COLS),
        in_specs=[pl.BlockSpec((GC_ROWS, GC_COLS), lambda i, j: (i, j)),
                  pl.BlockSpec((GC_HALO, GC_COLS), lambda i, j: (jnp.maximum(i * hb - 1, 0), j)),
                  pl.BlockSpec((GDN_CONV, GC_COLS), lambda i, j: (0, j))],
        out_specs=pl.BlockSpec((GC_ROWS, GC_COLS), lambda i, j: (i, j)),
        out_shape=jax.ShapeDtypeStruct((N, C), BF16),
        scratch_shapes=[pltpu.VMEM((GC_COLS // LANES, DW_PITCH * (GC_HALO + GC_ROWS), LANES), F32)],
        compiler_params=_cparams(("arbitrary", "arbitrary")), name="gdn_conv_silu",
    )(x, x, w)


GP_ROWS = 512
GP_SUB = 256


def _l2norm(t):
    return t * lax.rsqrt(jnp.sum(t * t, axis=-1, keepdims=True) + 1e-6)


def _gdn_prep_kernel(q_ref, k_ref, v_ref, braw_ref, araw_ref, alog_ref, dtb_ref,
                     u_ref, w_ref, qg_ref, kdt_ref, attn_ref, gl_ref):
    C = GDN_CHUNK
    S = GP_SUB
    ri = lax.broadcasted_iota(jnp.int32, (S, S), 0)
    ci = lax.broadcasted_iota(jnp.int32, (S, S), 1)
    same = _shr(ri, C) == _shr(ci, C)
    lower = same & (ri >= ci)
    strict = same & (ri > ci)
    eye = ri == ci
    cum_mat = (same & (ri <= ci)).astype(BF16)
    tot_mat = same.astype(BF16)
    eye_f = eye.astype(F32)

    def to_col(row):
        return jnp.sum(jnp.where(eye, jnp.broadcast_to(row, (S, S)), 0.0), axis=1, keepdims=True)

    n_sb = GP_ROWS // S
    chains = [(sb, e) for sb in range(n_sb) for e in range(2)]
    qs, ks, grams, qks = [], [], [], []
    for sb in range(n_sb):
        rows = slice(sb * S, (sb + 1) * S)
        q = _l2norm(q_ref[rows, :].astype(F32)) * (HEAD_DIM ** -0.5)
        k = _l2norm(k_ref[rows, :].astype(F32))
        kb16 = k.astype(BF16)
        qs.append(q)
        ks.append(k)
        both = _dot_nt(jnp.concatenate([kb16, q.astype(BF16)], axis=0), kb16)
        grams.append(both[:S])
        qks.append(both[S:])

    gcums, gtots, betas, decays, invs, pws = [], [], [], [], [], []
    for sb, e in chains:
        rows = slice(sb * S, (sb + 1) * S)
        a_row = araw_ref[e, :, rows]
        b_row = braw_ref[e, :, rows]
        g_row = -jnp.exp(alog_ref[e, :, 0:1]) * _softplus(a_row + dtb_ref[e, :, 0:1])
        g8 = jnp.broadcast_to(g_row, (SUBLANES, S))
        gcum_row = _dot_x3(g8, cum_mat)[0:1, :]
        gcum = to_col(gcum_row)
        gtots.append(to_col(_dot_x3(g8, tot_mat)[0:1, :]))
        beta = to_col(_sigmoid(b_row))
        decay = jnp.exp(jnp.where(lower, gcum - gcum_row, NEG_INF))
        a = jnp.where(strict, beta * grams[sb] * decay, 0.0)
        gcums.append(gcum)
        betas.append(beta)
        decays.append(decay)
        invs.append(eye_f - a)
        pws.append(a)

    for _ in range(int(math.log2(C)) - 1):
        pw16 = [pw.astype(BF16) for pw in pws]
        pws = [_dot(p16, p16) for p16 in pw16]
        invs = [inv + _dot(inv.astype(BF16), pw.astype(BF16)) for inv, pw in zip(invs, pws)]

    for ci_, (sb, e) in enumerate(chains):
        rows = slice(sb * S, (sb + 1) * S)
        cols = slice(e * HEAD_DIM, (e + 1) * HEAD_DIM)
        q, k = qs[sb], ks[sb]
        gcum, gtot, beta = gcums[ci_], gtots[ci_], betas[ci_]
        eg = jnp.exp(gcum)
        v = v_ref[rows, cols].astype(F32)
        inv16 = invs[ci_].astype(BF16)
        uw = _dot(inv16, jnp.concatenate([(v * beta).astype(BF16), (k * (beta * eg)).astype(BF16)], axis=1))
        u_ref[rows, cols] = uw[:, :HEAD_DIM]
        w_ref[rows, cols] = uw[:, HEAD_DIM:].astype(BF16)
        qg_ref[rows, cols] = (q * eg).astype(BF16)
        kdec = k * jnp.exp(gtot - gcum)
        kdt_ref[e, :, rows] = kdec.T.astype(BF16)
        attn = qks[sb] * decays[ci_]
        for c in range(S // C):
            attn_ref[e, sb * S + c * C:sb * S + (c + 1) * C, :] = (
                attn[c * C:(c + 1) * C, c * C:(c + 1) * C].astype(BF16))
            gl_ref[e, sb * (S // C) + c:sb * (S // C) + c + 1, :] = jnp.broadcast_to(
                jnp.exp(gtot[c * C:c * C + 1, :]), (1, LANES))


def gdn_prep(qkv, baT, a_log, dt_bias, B, T):
    N = qkv.shape[0]
    HK, HV, dh = GDN_QK_HEADS, GDN_V_HEADS, HEAD_DIM
    nt = T // GP_ROWS
    nc = GP_ROWS // GDN_CHUNK
    rowmap = lambda b, hk, t: (b * nt + t)
    alog = jnp.broadcast_to(a_log.astype(F32)[:, None, None], (HV, 1, LANES))
    dtb = jnp.broadcast_to(dt_bias.astype(F32)[:, None, None], (HV, 1, LANES))
    out_shape = [jax.ShapeDtypeStruct((N, HV * dh), F32),
                 jax.ShapeDtypeStruct((N, HV * dh), BF16),
                 jax.ShapeDtypeStruct((N, HV * dh), BF16),
                 jax.ShapeDtypeStruct((B, HV, dh, T), BF16),
                 jax.ShapeDtypeStruct((B, HV, T, GDN_CHUNK), BF16),
                 jax.ShapeDtypeStruct((B, HV, T // GDN_CHUNK, LANES), F32)]
    big = pl.BlockSpec((GP_ROWS, 2 * dh), lambda b, hk, t: (rowmap(b, hk, t), hk))
    return pl.pallas_call(
        _gdn_prep_kernel,
        grid=(B, HK, nt),
        in_specs=[pl.BlockSpec((GP_ROWS, dh), lambda b, hk, t: (rowmap(b, hk, t), hk)),
                  pl.BlockSpec((GP_ROWS, dh), lambda b, hk, t: (rowmap(b, hk, t), HK + hk)),
                  pl.BlockSpec((GP_ROWS, 2 * dh), lambda b, hk, t: (rowmap(b, hk, t), HK + hk)),
                  pl.BlockSpec((None, 2, 1, GP_ROWS), lambda b, hk, t: (b, hk, 0, t)),
                  pl.BlockSpec((None, 2, 1, GP_ROWS), lambda b, hk, t: (b, HK + hk, 0, t)),
                  pl.BlockSpec((2, 1, LANES), lambda b, hk, t: (hk, 0, 0)),
                  pl.BlockSpec((2, 1, LANES), lambda b, hk, t: (hk, 0, 0))],
        out_specs=[big, big, big,
                   pl.BlockSpec((None, 2, dh, GP_ROWS), lambda b, hk, t: (b, hk, 0, t)),
                   pl.BlockSpec((None, 2, GP_ROWS, GDN_CHUNK), lambda b, hk, t: (b, hk, t, 0)),
                   pl.BlockSpec((None, 2, nc, LANES), lambda b, hk, t: (b, hk, t, 0))],
        out_shape=out_shape,
        compiler_params=_cparams(("arbitrary", "arbitrary", "arbitrary")), name="gdn_prep",
    )(qkv, qkv, qkv, baT, baT, alog, dtb)


GS_HEADS = 4
GS_ROWS = 512


def _gdn_scan_kernel(u_ref, w_ref, qg_ref, kdt_ref, attn_ref, gl_ref, z_ref, nw_ref, o_ref, s_ref):
    C = GDN_CHUNK

    @pl.when(pl.program_id(2) == 0)
    def _():
        s_ref[...] = jnp.zeros_like(s_ref)

    nw = nw_ref[...]
    for c in range(GS_ROWS // C):
        rows = slice(c * C, (c + 1) * C)
        for hh in range(GS_HEADS):
            cols = slice(hh * HEAD_DIM, (hh + 1) * HEAD_DIM)
            s = s_ref[hh]
            lhs = jnp.concatenate([w_ref[rows, cols], qg_ref[rows, cols]], axis=0)
            r = _dot(lhs, s.astype(BF16))
            v_new = u_ref[rows, cols] - r[:C]
            lhs2 = jnp.concatenate([attn_ref[hh, rows, :], kdt_ref[hh, :, rows]], axis=0)
            r2 = _dot(lhs2, v_new.astype(BF16))
            o = r[C:] + r2[:C]
            s_ref[hh] = s * gl_ref[hh, c:c + 1, :] + r2[C:]
            z = z_ref[rows, cols].astype(F32)
            o = o * lax.rsqrt(jnp.mean(o * o, axis=-1, keepdims=True) + 1e-6) * nw * _silu(z)
            o_ref[rows, cols] = o.astype(o_ref.dtype)


def gdn_scan(u, w, qg, kdt, attn, gl, z, norm_w, B, T):
    N = u.shape[0]
    HV, dh = GDN_V_HEADS, HEAD_DIM
    nt = T // GS_ROWS
    nc = GS_ROWS // GDN_CHUNK
    wide = pl.BlockSpec((GS_ROWS, GS_HEADS * dh), lambda b, g, t: (b * nt + t, g))
    return pl.pallas_call(
        _gdn_scan_kernel,
        grid=(B, HV // GS_HEADS, nt),
        in_specs=[wide, wide, wide,
                  pl.BlockSpec((None, GS_HEADS, dh, GS_ROWS), lambda b, g, t: (b, g, 0, t)),
                  pl.BlockSpec((None, GS_HEADS, GS_ROWS, GDN_CHUNK), lambda b, g, t: (b, g, t, 0)),
                  pl.BlockSpec((None, GS_HEADS, nc, LANES), lambda b, g, t: (b, g, t, 0)),
                  wide,
                  pl.BlockSpec((1, dh), lambda b, g, t: (0, 0))],
        out_specs=wide,
        out_shape=jax.ShapeDtypeStruct((N, HV * dh), BF16),
        scratch_shapes=[pltpu.VMEM((GS_HEADS, dh, dh), F32)],
        compiler_params=_cparams(("arbitrary", "arbitrary", "arbitrary")), name="gdn_scan",
    )(u, w, qg, kdt, attn, gl, z, norm_w.astype(F32).reshape(1, dh))


def gdn_mixer(h, w_in, conv_w, a_log, dt_bias, norm_w, w_out, B, T):
    N, D = h.shape
    qkv_dim = 2 * GDN_QK_HEADS * HEAD_DIM + GDN_V_HEADS * HEAD_DIM
    v_dim = GDN_V_HEADS * HEAD_DIM
    w_qkv = w_in[:, :qkv_dim].astype(BF16)
    w_z = w_in[:, qkv_dim:qkv_dim + v_dim].astype(BF16)
    w_ba = jnp.zeros((D, LANES), BF16).at[:, :2 * GDN_V_HEADS].set(w_in[:, qkv_dim + v_dim:].astype(BF16))
    qkv = matmul(h, w_qkv, BF16, name="gdn_in_qkv")
    z = matmul(h, w_z, BF16, name="gdn_in_z")
    ba = matmul(h, w_ba, F32, name="gdn_in_ba")
    qkv = gdn_conv_silu(qkv, conv_w, T)
    baT = ba[:, :2 * GDN_V_HEADS].reshape(B, T, 2 * GDN_V_HEADS).transpose(0, 2, 1)[:, :, None, :]
    u, w, qg, kdt, attn, gl = gdn_prep(qkv, baT, a_log, dt_bias, B, T)
    o = gdn_scan(u, w, qg, kdt, attn, gl, z, norm_w, B, T)
    return matmul(o, w_out.astype(BF16), F32, name="gdn_out")


def _rel_bucket(dist):
    n = jnp.maximum(dist, 0)
    large = REL_EXACT + (jnp.log(jnp.maximum(n, 1).astype(F32) / REL_EXACT)
                         / math.log(REL_MAX_DIST / REL_EXACT) * (REL_BUCKETS - REL_EXACT)).astype(jnp.int32)
    return jnp.where(n < REL_EXACT, n, jnp.minimum(large, REL_BUCKETS - 1))


N_TOEP = 9
BIAS_PAD = 144


def _toeplitz(rows_rev):
    lead = rows_rev.shape[:-1]
    n = Q_BLOCK
    t = jnp.tile(rows_rev, (1,) * len(lead) + (n,))[..., :n * 2 * n].reshape(lead + (n, 2 * n))
    return t[..., n:]


def _nsa_bias_tables(rel_bias):
    n_max = Q_BLOCK * N_TOEP + BIAS_PAD
    dist = jnp.arange(-BIAS_PAD, n_max + 1)
    far = rel_bias.astype(F32)[REL_BUCKETS - 1]
    ftab = ((rel_bias.astype(F32)[_rel_bucket(dist)] - far[None, :]) * LOG2E).T

    def family(shift, n_d):
        rows = []
        for d in range(n_d):
            c = Q_BLOCK * d + Q_BLOCK + shift + BIAS_PAD
            rows.append(ftab[:, c - 2 * Q_BLOCK:c + 1][:, ::-1])
        return _toeplitz(jnp.stack(rows, axis=1))

    toep = family(0, N_TOEP)
    toep = toep.reshape(NSA_GROUPS, NSA_HPG, N_TOEP, Q_BLOCK, Q_BLOCK).transpose(0, 2, 1, 3, 4)
    toep = toep.reshape(NSA_GROUPS, N_TOEP, NSA_HPG * Q_BLOCK, Q_BLOCK)
    cm = family(-CMP_STRIDE, N_TOEP - 1)[..., ::CMP_STRIDE]
    strip = jnp.concatenate([cm[:, d] for d in range(N_TOEP - 2, -1, -1)], axis=-1)
    strip_hi = strip.astype(BF16)
    strip_lo = (strip - strip_hi.astype(F32)).astype(BF16)
    cmpb = jnp.concatenate([strip_hi, strip_lo], axis=-1).reshape(NSA_GROUPS, NSA_HPG * Q_BLOCK, LANES)
    return toep, cmpb


def _compress_kernel(x_ref, pos_ref, w1a_ref, w1b_ref, w2_ref, o_ref):
    x = x_ref[...].astype(F32)
    pos = pos_ref[...]
    a = _dot((x + pos[0:1, :]).astype(BF16), w1a_ref[...])
    b = _dot((x + pos[1:2, :]).astype(BF16), w1b_ref[...])
    n = x.shape[0]
    hid = a + pltpu.roll(b, n - 1, axis=0)
    o_ref[...] = _dot(_silu(hid).astype(BF16), w2_ref[...]).astype(o_ref.dtype)


def nsa_compress(kv_cmp, cmp_pos, w1k, w2k, w1v, w2v, B, T):
    G, dh = NSA_GROUPS, HEAD_DIM
    half = CMP_BLOCK // 2
    nch = T // half
    x = kv_cmp.reshape(B, nch, half, 2, G, dh).transpose(3, 0, 4, 1, 2, 5).reshape(2, B, G, nch, half * dh)
    pos = cmp_pos.astype(F32).reshape(2, 2, half * dh)
    w1 = jnp.stack([w1k, w1v]).astype(BF16)
    w2 = jnp.stack([w2k, w2v]).astype(BF16)
    return pl.pallas_call(
        _compress_kernel,
        grid=(2, B, G),
        in_specs=[pl.BlockSpec((None, None, None, nch, half * dh), lambda s, b, g: (s, b, g, 0, 0)),
                  pl.BlockSpec((None, 2, half * dh), lambda s, b, g: (s, 0, 0)),
                  pl.BlockSpec((None, half * dh, CMP_HIDDEN), lambda s, b, g: (s, 0, 0)),
                  pl.BlockSpec((None, half * dh, CMP_HIDDEN), lambda s, b, g: (s, 1, 0)),
                  pl.BlockSpec((None, CMP_HIDDEN, dh), lambda s, b, g: (s, 0, 0))],
        out_specs=pl.BlockSpec((None, None, None, nch, dh), lambda s, b, g: (s, b, g, 0, 0)),
        out_shape=jax.ShapeDtypeStruct((2, B, G, nch, dh), BF16),
        compiler_params=_cparams(("arbitrary", "arbitrary", "arbitrary")), name="nsa_compress",
    )(x, pos, w1, w1, w2)


SLC_TILE = 512
SM_ROWS = 64
WIN_KEYS = WINDOW + Q_BLOCK


def _nsa_attn_kernel(q_ref, kc_ref, vc_ref, ksa_ref, vs_ref, kw_ref, vw_ref, gate_ref, toep_ref, cmpb_ref,
                     ovl_ref, o_ref, qa_ref, s0_ref, s1_ref, p_ref, al_ref, oc_ref, ow_ref, acc_ref, m_ref, l_ref):
    qi = pl.program_id(2)
    qs = qi * Q_BLOCK
    HPG, dh, QB = NSA_HPG, HEAD_DIM, Q_BLOCK
    R = HPG * QB
    q = q_ref[...]
    Q = jnp.concatenate([q[:, h * dh:(h + 1) * dh] for h in range(HPG)], axis=0)
    tq = qs + lax.broadcasted_iota(jnp.int32, (QB, 1), 0)

    n_cmp = kc_ref.shape[0]
    strip_pos = lax.broadcasted_iota(jnp.int32, (n_cmp, LANES), 0) - 8 * (qi - (N_TOEP - 2))
    strip_lane = lax.broadcasted_iota(jnp.int32, (n_cmp, LANES), 1) & (LANES // 2 - 1)
    kca = jnp.concatenate([kc_ref[...], (strip_pos == strip_lane).astype(BF16)], axis=1)
    s_all = _dot_nt(jnp.concatenate([Q, cmpb_ref[...]], axis=1), kca)
    cidx = lax.broadcasted_iota(jnp.int32, (1, n_cmp), 1)
    valid = (cidx * CMP_STRIDE + (CMP_BLOCK - 1)) <= tq
    has_block = tq >= CMP_BLOCK - 1
    p4 = None
    pn16 = []
    for h in range(HPG):
        rows = slice(h * QB, (h + 1) * QB)
        s = jnp.where(valid, s_all[rows], NEG_INF)
        m = jnp.max(s, axis=1, keepdims=True)
        p = jnp.exp2(s - m)
        l = jnp.sum(p, axis=1, keepdims=True)
        pn = p * jnp.where(has_block, 1.0 / l, 0.0)
        p4 = pn if p4 is None else p4 + pn
        pn16.append(pn.astype(BF16))
    oc_ref[...] = _dot(jnp.concatenate(pn16, axis=0), vc_ref[...])

    imp = _dot_x2(p4, ovl_ref[...])
    n_slc = imp.shape[1]
    blk = lax.broadcasted_iota(jnp.int32, (QB, n_slc), 1)
    cur = _shr(tq, SLC_BLOCK)
    forced = (blk == 0) | (blk == cur) | (blk == cur - 1)
    n_forced = 3
    work = jnp.where(forced, -2.0, jnp.where(blk <= cur, imp, -1.0))
    work = work.T
    sidx = lax.broadcasted_iota(jnp.int32, (n_slc, QB), 0).astype(F32)
    sel = jnp.where(forced, 1.0, 0.0).T
    for _ in range(min(SLC_TOP_N, n_slc) - n_forced):
        mx = jnp.max(work, axis=0, keepdims=True)
        first = jnp.min(jnp.where(work == mx, sidx, float(n_slc)), axis=0, keepdims=True)
        hit = sidx == first
        sel = jnp.where(hit, 1.0, sel)
        work = jnp.where(hit, -2.0, work)
    selneg = jnp.where(sel > 0.5, 0.0, NEG_INF).T.astype(BF16)
    if n_slc < LANES:
        selneg = jnp.concatenate([selneg, jnp.zeros((QB, LANES - n_slc), BF16)], axis=1)
    qa_ref[:, 0:dh] = Q
    for h in range(HPG):
        qa_ref[h * QB:(h + 1) * QB, dh:] = selneg

    m_ref[...] = jnp.full(m_ref.shape, NEG_INF, F32)
    l_ref[...] = jnp.zeros(l_ref.shape, F32)
    acc_ref[...] = jnp.zeros(acc_ref.shape, F32)
    start = pl.multiple_of(jnp.maximum(qs - WINDOW, 0), QB)
    kwin = kw_ref[pl.ds(start, WIN_KEYS), :]
    vwin = vw_ref[pl.ds(start, WIN_KEYS), :]
    sw_all = _dot_nt(Q, kwin)
    doff = (qs - start) // QB
    n_wt = WIN_KEYS // QB
    wmask = []
    for t in range(n_wt):
        dist = tq - (start + t * QB + lax.broadcasted_iota(jnp.int32, (1, QB), 1))
        wmask.append(jnp.where((dist >= 0) & (dist < WINDOW), 0.0, NEG_INF))
    pw16 = []
    inv_lw = []
    for h in range(HPG):
        rows = slice(h * QB, (h + 1) * QB)
        parts = [sw_all[rows, t * QB:(t + 1) * QB] + toep_ref[jnp.clip(doff - t, 0, N_TOEP - 1), rows, :]
                 + wmask[t] for t in range(n_wt)]
        mw = parts[0]
        for t in range(1, n_wt):
            mw = jnp.maximum(mw, parts[t])
        mw = jnp.max(mw, axis=1, keepdims=True)
        pw = [jnp.exp2(pt - mw) for pt in parts]
        lw = pw[0]
        for t in range(1, n_wt):
            lw = lw + pw[t]
        inv_lw.append(1.0 / jnp.sum(lw, axis=1, keepdims=True))
        pw16.append(jnp.concatenate(pw, axis=1).astype(BF16))
    ow_ref[...] = _dot(jnp.concatenate(pw16, axis=0), vwin) * jnp.concatenate(inv_lw, axis=0)

    quarters = SLC_TILE // QB
    n_lane_tiles = SLC_TILE // LANES
    n_tiles = qi // quarters + 1
    last_tile = ksa_ref.shape[0] // SLC_TILE - 1
    j_near = jnp.maximum((qi - (N_TOEP - 2)) // quarters, 0)

    def scores(j, s_ref):
        k0 = pl.multiple_of(jnp.minimum(j, last_tile) * SLC_TILE, SLC_TILE)
        s_ref[...] = _dot_nt(qa_ref[...], ksa_ref[pl.ds(k0, SLC_TILE), :])

    def add_bias(j, s_ref):
        d0 = qi - quarters * j
        kpos = j * SLC_TILE + lax.broadcasted_iota(jnp.int32, (1, SLC_TILE), 1)
        causal = jnp.where(kpos <= tq, 0.0, NEG_INF)
        for h in range(HPG):
            rows = slice(h * QB, (h + 1) * QB)
            delta = jnp.concatenate(
                [toep_ref[jnp.clip(d0 - t, 0, N_TOEP - 1), rows, :] for t in range(quarters)], axis=1)
            s_ref[rows, :] = s_ref[rows, :] + delta + causal

    def softmax_pv(j, s_ref):
        for c in range(R // SM_ROWS):
            rows = slice(c * SM_ROWS, (c + 1) * SM_ROWS)
            parts = [s_ref[rows, t * LANES:(t + 1) * LANES] for t in range(n_lane_tiles)]
            mc = parts[0]
            for t in range(1, n_lane_tiles):
                mc = jnp.maximum(mc, parts[t])
            m_prev = m_ref[rows]
            m_new = jnp.maximum(m_prev, jnp.max(mc, axis=1, keepdims=True))
            alpha = jnp.exp2(m_prev - m_new)
            ps = [jnp.exp2(pt - m_new) for pt in parts]
            row_sum = ps[0]
            for t in range(1, n_lane_tiles):
                row_sum = row_sum + ps[t]
            l_ref[rows] = alpha * l_ref[rows] + jnp.sum(row_sum, axis=1, keepdims=True)
            m_ref[rows] = m_new
            al_ref[rows] = alpha
            for t in range(n_lane_tiles):
                p_ref[rows, t * LANES:(t + 1) * LANES] = ps[t].astype(BF16)
        k0 = pl.multiple_of(j * SLC_TILE, SLC_TILE)
        acc_ref[...] = al_ref[...] * acc_ref[...] + _dot(p_ref[...], vs_ref[pl.ds(k0, SLC_TILE), :])

    scores(0, s0_ref)

    def slc_pair(i, carry):
        ja = 2 * i
        jb = ja + 1

        @pl.when(ja >= j_near)
        def _():
            add_bias(ja, s0_ref)

        scores(jb, s1_ref)
        softmax_pv(ja, s0_ref)

        @pl.when((jb < n_tiles) & (jb >= j_near))
        def _():
            add_bias(jb, s1_ref)

        @pl.when(jb < n_tiles)
        def _():
            scores(jb + 1, s0_ref)
            softmax_pv(jb, s1_ref)

        return carry

    lax.fori_loop(0, (n_tiles + 1) // 2, slc_pair, 0)
    o_slc = acc_ref[...] / l_ref[...]

    gt = _sigmoid(gate_ref[...])
    o_cmp = oc_ref[...]
    o_win = ow_ref[...]
    outs = []
    for h in range(HPG):
        rows = slice(h * QB, (h + 1) * QB)
        outs.append(gt[:, 3 * h:3 * h + 1] * o_cmp[rows] + gt[:, 3 * h + 1:3 * h + 2] * o_slc[rows]
                    + gt[:, 3 * h + 2:3 * h + 3] * o_win[rows])
    o_ref[...] = jnp.concatenate(outs, axis=1).astype(o_ref.dtype)


def nsa_attention(q, kv, kvc, gates, toep, cmpb, B, T):
    N = q.shape[0]
    G, HPG, dh, QB = NSA_GROUPS, NSA_HPG, HEAD_DIM, Q_BLOCK
    nq = T // QB
    n_cmp = T // CMP_STRIDE
    n_slc = T // SLC_BLOCK
    c = jnp.arange(n_cmp)[:, None] * CMP_STRIDE
    sblk = jnp.arange(n_slc)[None, :] * SLC_BLOCK
    ovl = ((c < sblk + SLC_BLOCK) & (c + CMP_BLOCK > sblk)).astype(BF16)

    assert n_slc <= LANES
    member = (jnp.arange(T)[:, None] // SLC_BLOCK == jnp.arange(LANES)[None, :]).astype(BF16)
    k_slc = kv[:, :, 2 * G * dh:3 * G * dh].reshape(B, T, G, dh).transpose(0, 2, 1, 3)
    ksa = jnp.concatenate([k_slc, jnp.broadcast_to(member, (B, G, T, LANES))], axis=-1)

    def kv_spec(branch, which):
        col = (branch * 2 + which) * G
        return pl.BlockSpec((None, T, dh), lambda b, g, qi: (b, 0, col + g))

    def cmp_spec(which):
        return pl.BlockSpec((None, None, None, n_cmp, dh), lambda b, g, qi: (which, b, g, 0, 0))

    R = HPG * QB
    return pl.pallas_call(
        _nsa_attn_kernel,
        grid=(B, G, nq),
        in_specs=[pl.BlockSpec((QB, HPG * dh), lambda b, g, qi: (b * nq + qi, g)),
                  cmp_spec(0), cmp_spec(1),
                  pl.BlockSpec((None, None, T, dh + LANES), lambda b, g, qi: (b, g, 0, 0)),
                  kv_spec(1, 1), kv_spec(2, 0), kv_spec(2, 1),
                  pl.BlockSpec((QB, LANES), lambda b, g, qi: (b * nq + qi, g)),
                  pl.BlockSpec((None, N_TOEP, R, QB), lambda b, g, qi: (g, 0, 0, 0)),
                  pl.BlockSpec((None, R, LANES), lambda b, g, qi: (g, 0, 0)),
                  pl.BlockSpec((n_cmp, n_slc), lambda b, g, qi: (0, 0))],
        out_specs=pl.BlockSpec((QB, HPG * dh), lambda b, g, qi: (b * nq + qi, g)),
        out_shape=jax.ShapeDtypeStruct((N, G * HPG * dh), BF16),
        scratch_shapes=[pltpu.VMEM((R, dh + LANES), BF16),
                        pltpu.VMEM((R, SLC_TILE), F32),
                        pltpu.VMEM((R, SLC_TILE), F32),
                        pltpu.VMEM((R, SLC_TILE), BF16),
                        pltpu.VMEM((R, LANES), F32),
                        pltpu.VMEM((R, dh), F32),
                        pltpu.VMEM((R, dh), F32),
                        pltpu.VMEM((R, dh), F32), pltpu.VMEM((R, LANES), F32), pltpu.VMEM((R, LANES), F32)],
        compiler_params=_cparams(("arbitrary", "arbitrary", "arbitrary"), vmem_mib=56), name="nsa_attention",
    )(q, kvc, kvc, ksa, kv, kv, kv, gates, toep, cmpb, ovl)


def nsa_mixer(h, w_in, cmp_pos, w1k, w2k, w1v, w2v, w_o, toep, cmpb, B, T):
    N, D = h.shape
    G, HPG, dh = NSA_GROUPS, NSA_HPG, HEAD_DIM
    q_dim = NSA_HEADS * dh
    kv_dim = 3 * 2 * G * dh
    wq = w_in[:, :q_dim].astype(BF16)
    wkv = w_in[:, q_dim:q_dim + kv_dim].astype(BF16)
    wg = w_in[:, q_dim + kv_dim:].astype(BF16).reshape(D, G, HPG * 3)
    wg = jnp.zeros((D, G, LANES), BF16).at[:, :, :HPG * 3].set(wg).reshape(D, G * LANES)
    q = matmul(h, wq, BF16, scale=dh ** -0.5 * LOG2E, name="nsa_in_q")
    kv = matmul(h, wkv, BF16, name="nsa_in_kv")
    gates = matmul(h, wg, F32, name="nsa_in_gate")
    kv3 = kv.reshape(B, T, kv_dim)
    kvc = nsa_compress(kv3[:, :, :2 * G * dh], cmp_pos, w1k, w2k, w1v, w2v, B, T)
    o = nsa_attention(q, kv3, kvc, gates, toep, cmpb, B, T)
    return matmul(o, w_o.astype(BF16), F32, name="nsa_out")


def kernel(x, c, rel_bias, ada_w, ada_b, ln_g, ln_b, nsa_w_in, nsa_cmp_pos, nsa_cmp_w1k, nsa_cmp_w2k, nsa_cmp_w1v, nsa_cmp_w2v, nsa_w_o, conv_w_in, conv_dw, conv_ln_g, conv_ln_b, conv_w_out, gdn_w_in, gdn_conv, gdn_a_log, gdn_dt_bias, gdn_norm_w, gdn_w_out, ffn_w1, ffn_w3, ffn_w2, moe_router, moe_w1, moe_w3, moe_w2):
    B, T, D = x.shape
    N = B * T
    depth = ada_w.shape[0]
    mod = adaln_all(c, ada_w, ada_b).reshape(depth, B, 6, 1, D)
    toep, cmpb = _nsa_bias_tables(rel_bias)
    x2 = x.reshape(N, D)

    def h_dtype_for(layer, sub):
        return F32 if (sub == 1 and layer % 2 == 1) else BF16

    h = modulate(x2, mod[0, :, 1], mod[0, :, 0], T, h_dtype_for(0, 0))
    for i in range(depth):
        sh1, sc1, g1, sh2, sc2, g2 = (mod[i, :, k] for k in range(6))
        kind, j = i % 3, i // 3
        if kind == 0:
            y = nsa_mixer(h, nsa_w_in[j], nsa_cmp_pos[j], nsa_cmp_w1k[j], nsa_cmp_w2k[j], nsa_cmp_w1v[j],
                          nsa_cmp_w2v[j], nsa_w_o[j], toep, cmpb, B, T)
        elif kind == 1:
            y = conformer_mixer(h, conv_w_in[j], conv_dw[j], conv_ln_g[j], conv_ln_b[j], conv_w_out[j], T)
        else:
            y = gdn_mixer(h, gdn_w_in[j], gdn_conv[j], gdn_a_log[j], gdn_dt_bias[j], gdn_norm_w[j],
                          gdn_w_out[j], B, T)
        lg = ln_g[i].reshape(2, 1, D)
        lb = ln_b[i].reshape(2, 1, D)
        x2, h = norm_mod(x2, y, g1, lg[0], lb[0], sc2, sh2, T, h_dtype_for(i, 1))
        m = i // 2
        last = i == depth - 1
        sc_n = None if last else mod[i + 1, :, 1]
        sh_n = None if last else mod[i + 1, :, 0]
        if i % 2 == 0:
            y = ffn_swiglu(h, ffn_w1, ffn_w3, ffn_w2, m)
            x2, h = norm_mod(x2, y, g2, lg[1], lb[1], sc_n, sh_n, T, h_dtype_for(i + 1, 0))
        else:
            x2, h = moe_sublayer(h, x2, moe_router[m], moe_w1, moe_w3, moe_w2, m,
                                 g2, lg[1], lb[1], sc_n, sh_n, T, h_dtype_for(i + 1, 0))
    return x2.reshape(B, T, D)
```
